```python
import math
import jax
import jax.numpy as jnp
from jax import lax

D_MODEL = 1024
BATCH = 32
SEQ = 2048
DEPTH = 4
DEC_BATCH = 16
DEC_SEQ = 2048
PAST_LEN = 128

N_EVEN = (DEPTH + 1) // 2
N_ODD = DEPTH // 2
ALPHA = (2.0 * DEPTH) ** 0.25
BETA = (8.0 * DEPTH) ** -0.25
LN_EPS = 1e-5
D_FF = 2816
CONV_W = 4
CONV_PAD_LEFT = 2
ML_WIDTH = D_MODEL
ML_HEADS = 4
ML_DH = ML_WIDTH // ML_HEADS
ML_QKV_BLOCK = 4
ML_NBLK = ML_WIDTH // ML_QKV_BLOCK
ML_CHUNK = 128
LRU_WIDTH = D_MODEL
LRU_BLOCKS = 16
LRU_BDIM = LRU_WIDTH // LRU_BLOCKS
LRU_C = 8.0
HG_HEADS = 8
HG_DH = 128
HG_WIDTH = HG_HEADS * HG_DH
HG_CHUNK = 16
RET_HEADS = 4
RET_DK = 256
RET_DV = 256
RET_QK_WIDTH = RET_HEADS * RET_DK
RET_WIDTH = RET_HEADS * RET_DV
RET_CHUNK = 128
ROPE_BASE = 10000.0
EVEN_SIZES = [ML_WIDTH, ML_WIDTH, LRU_WIDTH, LRU_WIDTH]
ODD_SIZES = [HG_WIDTH] * 5 + [RET_QK_WIDTH, RET_QK_WIDTH, RET_WIDTH, RET_WIDTH]
EVEN_IN = sum(EVEN_SIZES)
ODD_IN = sum(ODD_SIZES)

kernel_name = 'hybrid_bidir_mlstm_rglru_hgrn2_retnet_encoder'


def _split(h, sizes):
    out, off = [], 0
    for s in sizes:
        out.append(h[..., off:off + s])
        off += s
    return out


def layer_norm(x, g, b):
    xf = x.astype(jnp.float32)
    mu = jnp.mean(xf, axis=-1, keepdims=True)
    var = jnp.mean(jnp.square(xf - mu), axis=-1, keepdims=True)
    return ((xf - mu) * lax.rsqrt(var + LN_EPS) * g + b).astype(x.dtype)


def head_norm(x, g, n_heads, center):
    B, L, W = x.shape
    xf = x.astype(jnp.float32).reshape(B, L, n_heads, W // n_heads)
    if center:
        xf = xf - jnp.mean(xf, axis=-1, keepdims=True)
    xf = xf * lax.rsqrt(jnp.mean(xf * xf, axis=-1, keepdims=True) + LN_EPS)
    return (xf.reshape(B, L, W) * g).astype(x.dtype)


def swiglu_ffn(x, w_in, w_out):
    gate, up = jnp.split(x @ w_in, 2, axis=-1)
    return (jax.nn.silu(gate) * up) @ w_out


def dw_conv(x, w, b):
    C = x.shape[-1]
    y = lax.conv_general_dilated(x, w[:, None, :], window_strides=(1,),
                                 padding=[(CONV_PAD_LEFT, CONV_W - 1 - CONV_PAD_LEFT)],
                                 dimension_numbers=('NWC', 'WIO', 'NWC'), feature_group_count=C)
    return y + b


def _to_chunks(t, chunk):
    B, H, L = t.shape[:3]
    return jnp.moveaxis(t.reshape(B, H, L // chunk, chunk, *t.shape[3:]), 2, 0)


def _from_chunks(t):
    t = jnp.moveaxis(t, 0, 2)
    return t.reshape(t.shape[0], t.shape[1], -1, *t.shape[4:])


def _flip_l(t):
    return jnp.flip(t, axis=2)


def mlstm_chunkwise(q, k, v, log_i, log_f):
    B, H, L, Dk = q.shape
    Dv = v.shape[-1]
    mask = jnp.tril(jnp.ones((ML_CHUNK, ML_CHUNK), dtype=bool))

    def step(carry, inp):
        c_st, n_st, m_st = carry
        qb, kb, vb, ib, fb = inp
        bcum = jnp.cumsum(fb, axis=-1)
        d_log = jnp.where(mask, bcum[..., :, None] - bcum[..., None, :] + ib[..., None, :], -jnp.inf)
        inter_log = bcum + m_st[..., None]
        m_t = jnp.maximum(inter_log, jnp.max(d_log, axis=-1))
        s = jnp.einsum('bhtk,bhsk->bhts', qb, kb) * jnp.exp(d_log - m_t[..., None])
        w_inter = jnp.exp(inter_log - m_t)
        num = jnp.einsum('bhts,bhsv->bhtv', s, vb) + w_inter[..., None] * jnp.einsum('bhtk,bhkv->bhtv', qb, c_st)
        den = jnp.sum(s, axis=-1) + w_inter * jnp.einsum('bhtk,bhk->bht', qb, n_st)
        h = num / jnp.maximum(jnp.abs(den), jnp.exp(-m_t))[..., None]
        b_last = bcum[..., -1]
        g_log = b_last[..., None] - bcum + ib
        m_new = jnp.maximum(b_last + m_st, jnp.max(g_log, axis=-1))
        w_k = jnp.exp(g_log - m_new[..., None])
        decay = jnp.exp(b_last + m_st - m_new)
        c_st = decay[..., None, None] * c_st + jnp.einsum('bhsk,bhsv->bhkv', kb * w_k[..., None], vb)
        n_st = decay[..., None] * n_st + jnp.einsum('bhsk,bhs->bhk', kb, w_k)
        return (c_st, n_st, m_new), h

    init = (jnp.zeros((B, H, Dk, Dv), jnp.float32), jnp.zeros((B, H, Dk), jnp.float32),
            jnp.zeros((B, H), jnp.float32))
    xs = tuple(_to_chunks(t, ML_CHUNK) for t in (q, k, v, log_i, log_f))
    _, h = lax.scan(step, init, xs)
    return _from_chunks(h)


def mlstm_mixer(u, z, conv_w, conv_b, wq, wk, wv, w_ig, b_ig, w_fg, b_fg, norm_g):
    B, L, _ = u.shape
    c = jax.nn.silu(dw_conv(u, conv_w, conv_b))

    def blockdiag(t, w):
        return jnp.einsum('blnc,ncd->blnd', t.reshape(B, L, ML_NBLK, ML_QKV_BLOCK), w).reshape(B, L, ML_WIDTH)

    q = blockdiag(c, wq)
    k = blockdiag(c, wk) / math.sqrt(ML_DH)
    v = blockdiag(u, wv)
    qkv = jnp.concatenate([q, k, v], axis=-1)
    i_pre = jnp.einsum('bld,rdh->rbhl', qkv, w_ig).astype(jnp.float32) + b_ig[:, None, :, None]
    log_f = jax.nn.log_sigmoid(jnp.einsum('bld,rdh->rbhl', qkv, w_fg).astype(jnp.float32) + b_fg[:, None, :, None])

    def heads(t):
        return t.reshape(B, L, ML_HEADS, ML_DH).transpose(0, 2, 1, 3).astype(jnp.float32)

    qh, kh, vh = heads(q), heads(k), heads(v)
    h_f = mlstm_chunkwise(qh, kh, vh, i_pre[0], log_f[0])
    h_b = _flip_l(mlstm_chunkwise(_flip_l(qh), _flip_l(kh), _flip_l(vh), _flip_l(i_pre[1]), _flip_l(log_f[1])))
    h = (h_f + h_b).transpose(0, 2, 1, 3).reshape(B, L, ML_WIDTH).astype(u.dtype)
    return jax.nn.sigmoid(z) * head_norm(h, norm_g, ML_HEADS, True)


def _linear_combine(e1, e2):
    a1, b1 = e1
    a2, b2 = e2
    return a1 * a2, a2 * b1 + b2


def rglru_direction(x, w_a, b_a, w_x, b_x, lam):
    B, L, W = x.shape
    xb = x.reshape(B, L, LRU_BLOCKS, LRU_BDIM)
    r = jax.nn.sigmoid((jnp.einsum('blnc,ncd->blnd', xb, w_a).reshape(B, L, W) + b_a).astype(jnp.float32))
    i = jax.nn.sigmoid((jnp.einsum('blnc,ncd->blnd', xb, w_x).reshape(B, L, W) + b_x).astype(jnp.float32))
    log_a = -LRU_C * r * jax.nn.softplus(-lam.astype(jnp.float32))
    a = jnp.exp(log_a)
    b = jnp.sqrt(-jnp.expm1(2.0 * log_a)) * i * x.astype(jnp.float32)
    _, h = lax.associative_scan(_linear_combine, (a, b), axis=1)
    return h


def rglru_mixer(u, g, conv_w, conv_b, w_a, b_a, w_x, b_x, lam):
    x = dw_conv(u, conv_w, conv_b)
    h_f = rglru_direction(x, w_a[0], b_a[0], w_x[0], b_x[0], lam[0])
    h_b = jnp.flip(rglru_direction(jnp.flip(x, 1), w_a[1], b_a[1], w_x[1], b_x[1], lam[1]), 1)
    return (h_f + h_b).astype(u.dtype) * jax.nn.gelu(g)


def gla_chunkwise(q, k, v, log_g):
    B, H, L, K = q.shape
    V = v.shape[-1]
    mask = jnp.tril(jnp.ones((HG_CHUNK, HG_CHUNK), dtype=bool))[:, :, None]

    def step(s_st, inp):
        qb, kb, vb, gb = inp
        bc = jnp.cumsum(gb, axis=2)
        diff = bc[:, :, :, None, :] - bc[:, :, None, :, :]
        wdec = jnp.exp(jnp.where(mask, diff, -jnp.inf))
        att = jnp.einsum('bhtk,bhtsk,bhsk->bhts', qb, wdec, kb)
        o = jnp.einsum('bhts,bhsv->bhtv', att, vb) + jnp.einsum('bhtk,bhkv->bhtv', qb * jnp.exp(bc), s_st)
        b_last = bc[:, :, -1]
        s_st = jnp.exp(b_last)[..., None] * s_st + jnp.einsum('bhsk,bhsv->bhkv', kb * jnp.exp(b_last[:, :, None] - bc), vb)
        return s_st, o

    init = jnp.zeros((B, H, K, V), jnp.float32)
    xs = tuple(_to_chunks(t, HG_CHUNK) for t in (q, k, v, log_g))
    _, o = lax.scan(step, init, xs)
    return _from_chunks(o)


def hgrn2_mixer(q, f_fwd, f_bwd, i, g, lb, norm_g):
    B, L, _ = q.shape

    def heads(t):
        return t.reshape(B, L, HG_HEADS, -1).transpose(0, 2, 1, 3).astype(jnp.float32)

    qh = heads(jax.nn.silu(q))
    vh = heads(i)
    f_f = lb + (1.0 - lb) * jax.nn.sigmoid(f_fwd.astype(jnp.float32))
    f_b = lb + (1.0 - lb) * jax.nn.sigmoid(f_bwd.astype(jnp.float32))
    o_f = gla_chunkwise(qh, heads(1.0 - f_f), vh, heads(jnp.log(f_f)))
    o_b = _flip_l(gla_chunkwise(_flip_l(qh), _flip_l(heads(1.0 - f_b)), _flip_l(vh), _flip_l(heads(jnp.log(f_b)))))
    o = (o_f + o_b).transpose(0, 2, 1, 3).reshape(B, L, HG_WIDTH).astype(q.dtype)
    return head_norm(o, norm_g, HG_HEADS, False) * jax.nn.silu(g)


def _retention_log_gammas():
    lg = jnp.log(1.0 - 2.0 ** (-5.0 - jnp.arange(RET_HEADS, dtype=jnp.float32)))
    return lg, lg[::-1]


def rope_tables(L):
    pos = jnp.arange(L, dtype=jnp.float32)
    inv = ROPE_BASE ** (-jnp.arange(RET_DK // 2, dtype=jnp.float32) / (RET_DK // 2))
    ang = pos[:, None] * inv[None, :]
    return jnp.cos(ang), jnp.sin(ang)


def apply_rope(t, cos, sin):
    half = t.shape[-1] // 2
    t1, t2 = t[..., :half], t[..., half:]
    return jnp.concatenate([t1 * cos - t2 * sin, t1 * sin + t2 * cos], axis=-1)


def retention_chunkwise(q, k, v, log_gamma):
    B, H, L, K = q.shape
    V = v.shape[-1]
    pos = jnp.arange(RET_CHUNK, dtype=jnp.float32)
    diff = pos[:, None] - pos[None, :]
    intra = jnp.exp(jnp.where(diff >= 0, diff[None] * log_gamma[:, None, None], -jnp.inf))
    q_scale = jnp.exp((pos[None, :] + 1.0) * log_gamma[:, None])
    k_scale = jnp.exp((RET_CHUNK - 1.0 - pos[None, :]) * log_gamma[:, None])
    chunk_decay = jnp.exp(RET_CHUNK * log_gamma)

    def step(r_st, inp):
        qb, kb, vb = inp
        att = jnp.einsum('bhtk,bhsk->bhts', qb, kb) * intra
        o = jnp.einsum('bhts,bhsv->bhtv', att, vb) + q_scale[..., None] * jnp.einsum('bhtk,bhkv->bhtv', qb, r_st)
        r_st = chunk_decay[:, None, None] * r_st + jnp.einsum('bhsk,bhsv->bhkv', kb * k_scale[..., None], vb)
        return r_st, o

    init = jnp.zeros((B, H, K, V), jnp.float32)
    xs = tuple(_to_chunks(t, RET_CHUNK) for t in (q, k, v))
    _, o = lax.scan(step, init, xs)
    return _from_chunks(o)


def retention_mixer(q, k, v, g, cos, sin, norm_g):
    B, L, _ = q.shape

    def heads(t):
        return t.reshape(B, L, RET_HEADS, -1).transpose(0, 2, 1, 3).astype(jnp.float32)

    qh = apply_rope(heads(q), cos, sin)
    kh = apply_rope(heads(k), cos, sin) / math.sqrt(RET_DK)
    vh = heads(v)
    lg_f, lg_b = _retention_log_gammas()
    o_f = retention_chunkwise(qh, kh, vh, lg_f)
    o_b = _flip_l(retention_chunkwise(_flip_l(qh), _flip_l(kh), _flip_l(vh), lg_b))
    o = (o_f + o_b).transpose(0, 2, 1, 3).reshape(B, L, RET_WIDTH).astype(q.dtype)
    return head_norm(o, norm_g, RET_HEADS, True) * jax.nn.silu(g)


def even_mixer(x, w_in, w_out, ml_conv_w, ml_conv_b, ml_wq, ml_wk, ml_wv, ml_w_ig, ml_b_ig,
               ml_w_fg, ml_b_fg, ml_norm_g, lru_conv_w, lru_conv_b, lru_w_a, lru_b_a, lru_w_x,
               lru_b_x, lru_lambda):
    u_m, z_m, u_r, g_r = _split(x @ w_in, EVEN_SIZES)
    y_m = mlstm_mixer(u_m, z_m, ml_conv_w, ml_conv_b, ml_wq, ml_wk, ml_wv, ml_w_ig, ml_b_ig,
                      ml_w_fg, ml_b_fg, ml_norm_g)
    y_r = rglru_mixer(u_r, g_r, lru_conv_w, lru_conv_b, lru_w_a, lru_b_a, lru_w_x, lru_b_x, lru_lambda)
    return jnp.concatenate([y_m, y_r], axis=-1) @ w_out


def odd_mixer(x, w_in, w_out, lb, hg_norm_g, ret_norm_g, cos, sin):
    q_c, f_f, f_b, i_c, g_c, q_r, k_r, v_r, g_r = _split(x @ w_in, ODD_SIZES)
    y_c = hgrn2_mixer(q_c, f_f, f_b, i_c, g_c, lb, hg_norm_g)
    y_r = retention_mixer(q_r, k_r, v_r, g_r, cos, sin, ret_norm_g)
    return jnp.concatenate([y_c, y_r], axis=-1) @ w_out


def setup_inputs(seed: int = 0) -> dict:
    key = jax.random.key(seed)
    ks = iter(jax.random.split(key, 48))
    f32 = jnp.float32

    def nrm(shape, scale):
        return jax.random.normal(next(ks), shape, f32) * scale

    x_prompt = nrm((BATCH, SEQ, D_MODEL), 1.0)
    x_sample = nrm((DEC_BATCH, DEC_SEQ, D_MODEL), 1.0)
    ffn_a_w_in = nrm((DEPTH, D_MODEL, 2 * D_FF), D_MODEL ** -0.5)
    ffn_a_w_out = nrm((DEPTH, D_FF, D_MODEL), BETA * D_FF ** -0.5)
    ffn_b_w_in = nrm((DEPTH, D_MODEL, 2 * D_FF), D_MODEL ** -0.5)
    ffn_b_w_out = nrm((DEPTH, D_FF, D_MODEL), BETA * D_FF ** -0.5)
    ln_g = 1.0 + nrm((DEPTH, 3, D_MODEL), 0.05)
    ln_b = nrm((DEPTH, 3, D_MODEL), 0.01)
    ev_w_in = nrm((N_EVEN, D_MODEL, EVEN_IN), D_MODEL ** -0.5)
    ev_w_out = nrm((N_EVEN, ML_WIDTH + LRU_WIDTH, D_MODEL), BETA * (ML_WIDTH + LRU_WIDTH) ** -0.5)
    ml_conv_w = nrm((N_EVEN, CONV_W, ML_WIDTH), CONV_W ** -0.5)
    ml_conv_b = nrm((N_EVEN, ML_WIDTH), 0.01)
    ml_wq = nrm((N_EVEN, ML_NBLK, ML_QKV_BLOCK, ML_QKV_BLOCK), ML_QKV_BLOCK ** -0.5)
    ml_wk = nrm((N_EVEN, ML_NBLK, ML_QKV_BLOCK, ML_QKV_BLOCK), ML_QKV_BLOCK ** -0.5)
    ml_wv = nrm((N_EVEN, ML_NBLK, ML_QKV_BLOCK, ML_QKV_BLOCK), ML_QKV_BLOCK ** -0.5)
    ml_w_ig = nrm((N_EVEN, 2, 3 * ML_WIDTH, ML_HEADS), 0.1 * (3 * ML_WIDTH) ** -0.5)
    ml_b_ig = nrm((N_EVEN, 2, ML_HEADS), 0.1)
    ml_w_fg = nrm((N_EVEN, 2, 3 * ML_WIDTH, ML_HEADS), 0.1 * (3 * ML_WIDTH) ** -0.5)
    ml_b_fg = jnp.linspace(3.0, 6.0, ML_HEADS, dtype=f32) + nrm((N_EVEN, 2, ML_HEADS), 0.1)
    ml_norm_g = 1.0 + nrm((N_EVEN, ML_WIDTH), 0.05)
    lru_conv_w = nrm((N_EVEN, CONV_W, LRU_WIDTH), CONV_W ** -0.5)
    lru_conv_b = nrm((N_EVEN, LRU_WIDTH), 0.01)
    lru_w_a = nrm((N_EVEN, 2, LRU_BLOCKS, LRU_BDIM, LRU_BDIM), LRU_BDIM ** -0.5)
    lru_b_a = nrm((N_EVEN, 2, LRU_WIDTH), 0.01)
    lru_w_x = nrm((N_EVEN, 2, LRU_BLOCKS, LRU_BDIM, LRU_BDIM), LRU_BDIM ** -0.5)
    lru_b_x = nrm((N_EVEN, 2, LRU_WIDTH), 0.01)
    a0 = jax.random.uniform(next(ks), (N_EVEN, 2, LRU_WIDTH), f32, 0.9, 0.999)
    p = a0 ** (1.0 / LRU_C)
    lru_lambda = jnp.log(p) - jnp.log1p(-p)
    od_w_in = nrm((N_ODD, D_MODEL, ODD_IN), D_MODEL ** -0.5)
    od_w_out = nrm((N_ODD, HG_WIDTH + RET_WIDTH, D_MODEL), BETA * (HG_WIDTH + RET_WIDTH) ** -0.5)
    hg_lb_logits = nrm((DEPTH, HG_WIDTH), 0.1)
    hg_norm_g = 1.0 + nrm((N_ODD, HG_WIDTH), 0.05)
    ret_norm_g = 1.0 + nrm((N_ODD, RET_WIDTH), 0.05)
    return {'x_prompt': x_prompt, 'x_sample': x_sample,
            'ffn_a_w_in': ffn_a_w_in, 'ffn_a_w_out': ffn_a_w_out,
            'ffn_b_w_in': ffn_b_w_in, 'ffn_b_w_out': ffn_b_w_out,
            'ln_g': ln_g, 'ln_b': ln_b,
            'ev_w_in': ev_w_in, 'ev_w_out': ev_w_out,
            'ml_conv_w': ml_conv_w, 'ml_conv_b': ml_conv_b,
            'ml_wq': ml_wq, 'ml_wk': ml_wk, 'ml_wv': ml_wv,
            'ml_w_ig': ml_w_ig, 'ml_b_ig': ml_b_ig, 'ml_w_fg': ml_w_fg, 'ml_b_fg': ml_b_fg,
            'ml_norm_g': ml_norm_g,
            'lru_conv_w': lru_conv_w, 'lru_conv_b': lru_conv_b,
            'lru_w_a': lru_w_a, 'lru_b_a': lru_b_a, 'lru_w_x': lru_w_x, 'lru_b_x': lru_b_x,
            'lru_lambda': lru_lambda,
            'od_w_in': od_w_in, 'od_w_out': od_w_out,
            'hg_lb_logits': hg_lb_logits, 'hg_norm_g': hg_norm_g, 'ret_norm_g': ret_norm_g}


def reference(x_prompt, x_sample, ffn_a_w_in, ffn_a_w_out, ffn_b_w_in, ffn_b_w_out, ln_g, ln_b,
              ev_w_in, ev_w_out, ml_conv_w, ml_conv_b, ml_wq, ml_wk, ml_wv, ml_w_ig, ml_b_ig,
              ml_w_fg, ml_b_fg, ml_norm_g, lru_conv_w, lru_conv_b, lru_w_a, lru_b_a, lru_w_x,
              lru_b_x, lru_lambda, od_w_in, od_w_out, hg_lb_logits, hg_norm_g, ret_norm_g):
    lbs = jnp.cumsum(jax.nn.softmax(hg_lb_logits.astype(jnp.float32), axis=0), axis=0)
    lbs = lbs - lbs[0]

    def trunk(x):
        cos, sin = rope_tables(x.shape[1])
        for l in range(DEPTH):
            x = layer_norm(ALPHA * x + 0.5 * swiglu_ffn(x, ffn_a_w_in[l], ffn_a_w_out[l]), ln_g[l, 0], ln_b[l, 0])
            if l % 2 == 0:
                e = l // 2
                mix = even_mixer(x, ev_w_in[e], ev_w_out[e], ml_conv_w[e], ml_conv_b[e], ml_wq[e], ml_wk[e],
                                 ml_wv[e], ml_w_ig[e], ml_b_ig[e], ml_w_fg[e], ml_b_fg[e], ml_norm_g[e],
                                 lru_conv_w[e], lru_conv_b[e], lru_w_a[e], lru_b_a[e], lru_w_x[e],
                                 lru_b_x[e], lru_lambda[e])
            else:
                o = l // 2
                mix = odd_mixer(x, od_w_in[o], od_w_out[o], lbs[l], hg_norm_g[o], ret_norm_g[o], cos, sin)
            x = layer_norm(ALPHA * x + mix, ln_g[l, 1], ln_b[l, 1])
            x = layer_norm(ALPHA * x + 0.5 * swiglu_ffn(x, ffn_b_w_in[l], ffn_b_w_out[l]), ln_g[l, 2], ln_b[l, 2])
        return x

    y_prompt = trunk(x_prompt)
    y_sample = trunk(x_sample)
    return (y_prompt, y_sample)
```

```python
import functools
import math

import jax
import jax.numpy as jnp
from jax import lax
from jax.experimental import pallas as pl
from jax.experimental.pallas import tpu as pltpu

F32 = jnp.float32
BF16 = jnp.bfloat16

LN_EPS = 1e-5
CONV_W = 4
CONV_PAD_LEFT = 2
ML_HEADS = 4
ML_QKV_BLOCK = 4
LRU_BLOCKS = 16
LRU_C = 8.0
HG_HEADS = 8
RET_HEADS = 4
ROPE_BASE = 10000.0

LANES = 128
MXU_DIM = 256

TOKEN_TILE = 512
FFN_CHUNK = 1408
ML_CHUNK = MXU_DIM
RET_CHUNK = MXU_DIM
HG_CHUNK = LANES
VMEM_LIMIT = 56 * 1024 * 1024


def _mm(a, b):
    return jnp.dot(a, b, preferred_element_type=F32)


def _mm_nt(a, b):
    return lax.dot_general(a, b, (((1,), (1,)), ((), ())), preferred_element_type=F32)


def _mm_tn(a, b):
    return lax.dot_general(a, b, (((0,), (0,)), ((), ())), preferred_element_type=F32)


def _split3(x):
    x1 = x.astype(BF16)
    r = x - x1.astype(F32)
    x2 = r.astype(BF16)
    r = r - x2.astype(F32)
    return x1, x2, r.astype(BF16)


def _tri_left(tri, x):
    x1, x2, x3 = _split3(x)
    return _mm(tri, x1) + _mm(tri, x2) + _mm(tri, x3)


def _tri_right(x, tri):
    x1, x2, x3 = _split3(x)
    return _mm(x1, tri) + _mm(x2, tri) + _mm(x3, tri)


def _tri(n, lower):
    r = lax.broadcasted_iota(jnp.int32, (n, n), 0)
    c = lax.broadcasted_iota(jnp.int32, (n, n), 1)
    return r >= c if lower else r <= c


def _sigmoid(x):
    return 1.0 / (1.0 + jnp.exp(-x))


def _silu(x):
    return x * _sigmoid(x)


def _log_sigmoid(x):
    return jnp.minimum(x, 0.0) - jnp.log(1.0 + jnp.exp(-jnp.abs(x)))


def _layer_norm(y, g, b):
    mu = jnp.mean(y, axis=-1, keepdims=True)
    d = y - mu
    var = jnp.mean(d * d, axis=-1, keepdims=True)
    return d * lax.rsqrt(var + LN_EPS) * g + b


def _dwconv(u, w, b):
    n = u.shape[0]
    row = lax.broadcasted_iota(jnp.int32, u.shape, 0)
    acc = u * w[CONV_PAD_LEFT:CONV_PAD_LEFT + 1] + b
    for j in range(CONV_W):
        off = j - CONV_PAD_LEFT
        if off == 0:
            continue
        shifted = pltpu.roll(u, (-off) % n, 0)
        valid = (row + off >= 0) & (row + off < n)
        acc = acc + jnp.where(valid, shifted, 0.0) * w[j:j + 1]
    return acc


def _col(x, idx):
    lane = lax.broadcasted_iota(jnp.int32, x.shape, 1)
    return jnp.sum(jnp.where(lane == idx, x, 0.0), axis=1, keepdims=True)


def _row(x, idx):
    sub = lax.broadcasted_iota(jnp.int32, x.shape, 0)
    return jnp.sum(jnp.where(sub == idx, x, 0.0), axis=0, keepdims=True)


def _params(*sem):
    return pltpu.CompilerParams(dimension_semantics=sem, vmem_limit_bytes=VMEM_LIMIT)


def _lbs_kernel(logit_ref, o_ref):
    z = logit_ref[...]
    e = jnp.exp(z - jnp.max(z, axis=0, keepdims=True))
    p = e / jnp.sum(e, axis=0, keepdims=True)
    depth = z.shape[0]
    run = jnp.zeros_like(p[0:1])
    for l in range(1, depth):
        run = run + p[l:l + 1]
        o_ref[l:l + 1, :] = run
    o_ref[0:1, :] = jnp.zeros_like(run)


def _lower_bounds(logits):
    return pl.pallas_call(_lbs_kernel, out_shape=jax.ShapeDtypeStruct(logits.shape, F32))(logits.astype(F32))


def _ffn_kernel(x_ref, win_ref, wout_ref, g_ref, b_ref, *o_refs, d_ff, alpha):
    x = x_ref[...]
    xb = x.astype(BF16)
    acc = jnp.zeros(x.shape, F32)
    for c0 in range(0, d_ff, FFN_CHUNK):
        gate = _mm(xb, win_ref[:, c0:c0 + FFN_CHUNK])
        up = _mm(xb, win_ref[:, d_ff + c0:d_ff + c0 + FFN_CHUNK])
        h = (_silu(gate) * up).astype(BF16)
        acc = acc + _mm(h, wout_ref[c0:c0 + FFN_CHUNK, :])
    y = _layer_norm(alpha * x + 0.5 * acc, g_ref[...], b_ref[...])
    o_refs[0][...] = y
    if len(o_refs) > 1:
        o_refs[1][...] = y.astype(BF16)


def _ffn_ln(x, w_in, w_out, layer, ln_g, ln_b, ln_idx, alpha, with_bf16):
    n, d = x.shape
    d_ff = w_out.shape[1]
    tm = min(TOKEN_TILE, n)
    assert n % tm == 0 and d_ff % FFN_CHUNK == 0
    out_shape = [jax.ShapeDtypeStruct((n, d), F32)]
    out_specs = [pl.BlockSpec((tm, d), lambda i: (i, 0))]
    if with_bf16:
        out_shape.append(jax.ShapeDtypeStruct((n, d), BF16))
        out_specs.append(pl.BlockSpec((tm, d), lambda i: (i, 0)))
    const = dict(pipeline_mode=pl.Buffered(1))
    return pl.pallas_call(
        functools.partial(_ffn_kernel, d_ff=d_ff, alpha=alpha),
        grid=(n // tm,),
        in_specs=[
            pl.BlockSpec((tm, d), lambda i: (i, 0)),
            pl.BlockSpec((None, d, 2 * d_ff), lambda i: (layer, 0, 0), **const),
            pl.BlockSpec((None, d_ff, d), lambda i: (layer, 0, 0), **const),
            pl.BlockSpec((None, 1, d), lambda i: (layer * 3 + ln_idx, 0, 0)),
            pl.BlockSpec((None, 1, d), lambda i: (layer * 3 + ln_idx, 0, 0)),
        ],
        out_specs=out_specs,
        out_shape=out_shape,
        compiler_params=_params("parallel"),
    )(x, w_in, w_out, ln_g, ln_b)


def _proj_ln_kernel(x_ref, y1_ref, y2_ref, w_ref, g_ref, b_ref, o_ref, *, alpha):
    w1 = y1_ref.shape[1]
    acc = _mm(y1_ref[...], w_ref[0:w1, :]) + _mm(y2_ref[...], w_ref[w1:, :])
    o_ref[...] = _layer_norm(alpha * x_ref[...] + acc, g_ref[...], b_ref[...])


def _proj_ln(x, y1, y2, w_out, idx, ln_g, ln_b, ln_row, alpha):
    n, d = x.shape
    w1, w2 = y1.shape[1], y2.shape[1]
    tm = min(TOKEN_TILE, n)
    return pl.pallas_call(
        functools.partial(_proj_ln_kernel, alpha=alpha),
        grid=(n // tm,),
        in_specs=[
            pl.BlockSpec((tm, d), lambda i: (i, 0)),
            pl.BlockSpec((tm, w1), lambda i: (i, 0)),
            pl.BlockSpec((tm, w2), lambda i: (i, 0)),
            pl.BlockSpec((None, w1 + w2, d), lambda i: (idx, 0, 0)),
            pl.BlockSpec((None, 1, d), lambda i: (ln_row, 0, 0)),
            pl.BlockSpec((None, 1, d), lambda i: (ln_row, 0, 0)),
        ],
        out_specs=pl.BlockSpec((tm, d), lambda i: (i, 0)),
        out_shape=jax.ShapeDtypeStruct((n, d), F32),
        compiler_params=_params("parallel"),
    )(x, y1, y2, w_out, ln_g, ln_b)


def _ml_prep_kernel(xb_ref, wu_ref, cw_ref, cb_ref, wq_ref, wk_ref, wv_ref, gw_ref, gwt_ref, bcol_ref, brow_ref,
                    q_ref, k_ref, v_ref, gc_ref, gr_ref):
    u = _mm(xb_ref[...], wu_ref[...])
    c = _silu(_dwconv(u, cw_ref[...], cb_ref[...])).astype(BF16)
    qb = _mm(c, wq_ref[...]).astype(BF16)
    kb = _mm(c, wk_ref[...]).astype(BF16)
    vb = _mm(u.astype(BF16), wv_ref[...]).astype(BF16)
    q_ref[...] = qb
    k_ref[...] = kb
    v_ref[...] = vb

    @pl.when(pl.program_id(1) == 0)
    def _():
        gc_ref[...] = jnp.broadcast_to(bcol_ref[...], gc_ref.shape)
        gr_ref[...] = jnp.broadcast_to(brow_ref[...], gr_ref.shape)

    gc_ref[...] += _mm(qb, gw_ref[0]) + _mm(kb, gw_ref[1]) + _mm(vb, gw_ref[2])
    gr_ref[...] += _mm_nt(gwt_ref[0], qb) + _mm_nt(gwt_ref[1], kb) + _mm_nt(gwt_ref[2], vb)


def _ml_prep(xb, w_in, e, cw, cb, wq, wk, wv, gw, gwt, bcol, brow):
    bsz, seq, d = xb.shape
    dh = wq.shape[-1]
    heads = ML_HEADS
    ng = gwt.shape[-2]
    qkv = jax.ShapeDtypeStruct((bsz, seq, heads * dh), BF16)
    head_blk = pl.BlockSpec((None, seq, dh), lambda b, h: (b, 0, h))
    return pl.pallas_call(
        _ml_prep_kernel,
        grid=(bsz, heads),
        in_specs=[
            pl.BlockSpec((None, seq, d), lambda b, h: (b, 0, 0)),
            pl.BlockSpec((None, d, dh), lambda b, h: (e, 0, h)),
            pl.BlockSpec((None, CONV_W, dh), lambda b, h: (e, 0, h)),
            pl.BlockSpec((None, 1, dh), lambda b, h: (e, 0, h)),
            pl.BlockSpec((None, None, dh, dh), lambda b, h: (e, h, 0, 0)),
            pl.BlockSpec((None, None, dh, dh), lambda b, h: (e, h, 0, 0)),
            pl.BlockSpec((None, None, dh, dh), lambda b, h: (e, h, 0, 0)),
            pl.BlockSpec((None, None, 3, dh, LANES), lambda b, h: (e, h, 0, 0, 0)),
            pl.BlockSpec((None, None, 3, ng, dh), lambda b, h: (e, h, 0, 0, 0)),
            pl.BlockSpec((None, 1, LANES), lambda b, h: (e, 0, 0)),
            pl.BlockSpec((None, ng, 1), lambda b, h: (e, 0, 0)),
        ],
        out_specs=[head_blk, head_blk, head_blk,
                   pl.BlockSpec((None, seq, LANES), lambda b, h: (b, 0, 0)),
                   pl.BlockSpec((None, ng, seq), lambda b, h: (b, 0, 0))],
        out_shape=[qkv, qkv, qkv,
                   jax.ShapeDtypeStruct((bsz, seq, LANES), F32),
                   jax.ShapeDtypeStruct((bsz, ng, seq), F32)],
        compiler_params=_params("parallel", "arbitrary"),
    )(xb, w_in, cw, cb, wq, wk, wv, gw, gwt, bcol, brow)


def _ml_core_kernel(q_ref, k_ref, v_ref, gc_ref, gr_ref, xb_ref, wz_ref, ng_ref, y_ref, h_ref, c_ref, n_ref):
    head = pl.program_id(1)
    seq, dh = q_ref.shape
    ck = ML_CHUNK
    nchunks = seq // ck
    tri_l = _tri(ck, True)
    tri_u = _tri(ck, False)
    tri_l_b = jnp.where(tri_l, 1.0, 0.0).astype(BF16)
    tri_u_b = jnp.where(tri_u, 1.0, 0.0).astype(BF16)

    def run(backward):
        i_idx = head * 4 + (1 if backward else 0)
        f_idx = head * 4 + (3 if backward else 2)
        left = tri_u_b if backward else tri_l_b
        right = tri_l_b if backward else tri_u_b
        mask = tri_u if backward else tri_l
        far = 0 if backward else ck - 1
        c_ref[...] = jnp.zeros(c_ref.shape, F32)
        n_ref[...] = jnp.zeros(n_ref.shape, F32)

        def step(it, m_prev):
            j = (nchunks - 1 - it) if backward else it
            r0 = pl.multiple_of(j * ck, ck)
            qc = q_ref[pl.ds(r0, ck), :]
            kc = k_ref[pl.ds(r0, ck), :]
            vc = v_ref[pl.ds(r0, ck), :]
            gcol = gc_ref[pl.ds(r0, ck), :]
            grow = gr_ref[j]
            cum_col = _col(_tri_left(left, _log_sigmoid(gcol)), f_idx)
            cum_row = _row(_tri_right(_log_sigmoid(grow), right), f_idx)
            i_col = _col(gcol, i_idx)
            i_row = _row(grow, i_idx)
            total = cum_row[:, far:far + 1]

            d_log = jnp.where(mask, cum_col - cum_row + i_row, -jnp.inf)
            inter_log = cum_col + m_prev
            m_t = jnp.maximum(inter_log, jnp.max(d_log, axis=1, keepdims=True))
            p = _mm_nt(qc, kc) * jnp.exp(d_log - m_t)
            w_inter = jnp.exp(inter_log - m_t)
            num = _mm(p.astype(BF16), vc) + w_inter * _mm(qc, c_ref[...].astype(BF16))
            qn = jnp.sum(qc.astype(F32) * n_ref[...], axis=1, keepdims=True)
            den = jnp.sum(p, axis=1, keepdims=True) + w_inter * qn
            hc = num / jnp.maximum(jnp.abs(den), jnp.exp(-m_t))
            if backward:
                h_ref[pl.ds(r0, ck), :] += hc
            else:
                h_ref[pl.ds(r0, ck), :] = hc

            g_col = total - cum_col + i_col
            g_row = total - cum_row + i_row
            m_new = jnp.maximum(total + m_prev, jnp.max(g_row, axis=1, keepdims=True))
            kw = kc.astype(F32) * jnp.exp(g_col - m_new)
            decay = jnp.exp(total + m_prev - m_new)
            c_ref[...] = decay * c_ref[...] + _mm_tn(kw.astype(BF16), vc)
            n_ref[...] = decay * n_ref[...] + jnp.sum(kw, axis=0, keepdims=True)
            return m_new

        lax.fori_loop(0, nchunks, step, jnp.zeros((1, 1), F32))

    run(False)
    run(True)
    h = h_ref[...]
    h = h - jnp.mean(h, axis=-1, keepdims=True)
    h = h * lax.rsqrt(jnp.mean(h * h, axis=-1, keepdims=True) + LN_EPS) * ng_ref[...]
    z = _mm(xb_ref[...], wz_ref[...])
    y_ref[...] = (_sigmoid(z) * h).astype(BF16)


def _ml_core(q, k, v, gc, gr, xb, w_in, e, norm_g):
    bsz, seq, width = q.shape
    d = xb.shape[-1]
    heads = ML_HEADS
    dh = width // heads
    nchunks = seq // ML_CHUNK
    ng = gr.shape[1]
    gr = gr.reshape(bsz, ng, nchunks, ML_CHUNK).transpose(0, 2, 1, 3)
    head_blk = pl.BlockSpec((None, seq, dh), lambda b, h: (b, 0, h))
    return pl.pallas_call(
        _ml_core_kernel,
        grid=(bsz, heads),
        in_specs=[
            head_blk, head_blk, head_blk,
            pl.BlockSpec((None, seq, LANES), lambda b, h: (b, 0, 0)),
            pl.BlockSpec((None, nchunks, ng, ML_CHUNK), lambda b, h: (b, 0, 0, 0)),
            pl.BlockSpec((None, seq, d), lambda b, h: (b, 0, 0)),
            pl.BlockSpec((None, d, dh), lambda b, h: (e, 0, heads + h)),
            pl.BlockSpec((None, 1, dh), lambda b, h: (e, 0, h)),
        ],
        out_specs=head_blk,
        out_shape=jax.ShapeDtypeStruct((bsz, seq, width), BF16),
        scratch_shapes=[pltpu.VMEM((seq, dh), F32), pltpu.VMEM((dh, dh), F32), pltpu.VMEM((1, dh), F32)],
        compiler_params=_params("parallel", "arbitrary"),
    )(q, k, v, gc, gr, xb, w_in, norm_g)


def _scan_rows(a, b, backward):
    n = a.shape[0]
    row = lax.broadcasted_iota(jnp.int32, a.shape, 0)
    s = 1
    while s < n:
        if backward:
            keep = row < n - s
            shift = n - s
        else:
            keep = row >= s
            shift = s
        a_sh = jnp.where(keep, pltpu.roll(a, shift, 0), 1.0)
        b_sh = jnp.where(keep, pltpu.roll(b, shift, 0), 0.0)
        b = b + a * b_sh
        a = a * a_sh
        s *= 2
    return b


def _lru_kernel(xb_ref, wu_ref, wg_ref, cw_ref, cb_ref, wgate_ref, bgate_ref, lam_ref, y_ref):
    w = wu_ref.shape[1]
    xb = xb_ref[...]
    u = _mm(xb, wu_ref[...])
    x = _dwconv(u, cw_ref[...], cb_ref[...])
    pre = _mm(x.astype(BF16), wgate_ref[...]) + bgate_ref[...]
    lam = lam_ref[...]
    sp = jnp.maximum(-lam, 0.0) + jnp.log(1.0 + jnp.exp(-jnp.abs(lam)))
    h = None
    for dirn in range(2):
        r = _sigmoid(pre[:, (2 * dirn) * w:(2 * dirn + 1) * w])
        i = _sigmoid(pre[:, (2 * dirn + 1) * w:(2 * dirn + 2) * w])
        log_a = -LRU_C * r * sp[dirn:dirn + 1]
        a = jnp.exp(log_a)
        b = jnp.sqrt(1.0 - jnp.exp(2.0 * log_a)) * i * x
        hd = _scan_rows(a, b, backward=(dirn == 1))
        h = hd if h is None else h + hd
    g = _mm(xb, wg_ref[...])
    y_ref[...] = (h * jax.nn.gelu(g, approximate=True)).astype(BF16)


def _lru(xb, w_in, e, cw, cb, wgate, bgate, lam):
    bsz, seq, d = xb.shape
    groups, w = wgate.shape[1], wgate.shape[2]
    ublk = 2 * ML_HEADS
    return pl.pallas_call(
        _lru_kernel,
        grid=(bsz, groups),
        in_specs=[
            pl.BlockSpec((None, seq, d), lambda b, j: (b, 0, 0)),
            pl.BlockSpec((None, d, w), lambda b, j: (e, 0, ublk + j)),
            pl.BlockSpec((None, d, w), lambda b, j: (e, 0, ublk + groups + j)),
            pl.BlockSpec((None, CONV_W, w), lambda b, j: (e, 0, j)),
            pl.BlockSpec((None, 1, w), lambda b, j: (e, 0, j)),
            pl.BlockSpec((None, None, w, 4 * w), lambda b, j: (e, j, 0, 0)),
            pl.BlockSpec((None, None, 1, 4 * w), lambda b, j: (e, j, 0, 0)),
            pl.BlockSpec((None, 2, w), lambda b, j: (e, 0, j)),
        ],
        out_specs=pl.BlockSpec((None, seq, w), lambda b, j: (b, 0, j)),
        out_shape=jax.ShapeDtypeStruct((bsz, seq, groups * w), BF16),
        compiler_params=_params("parallel", "arbitrary"),
    )(xb, w_in, w_in, cw, cb, wgate, bgate, lam)


def _hg_kernel(xb_ref, w_ref, lb_ref, ng_ref, y_ref, q_ref, k_ref, g_ref, v_ref, o_ref, s_ref):
    seq, dk = q_ref.shape
    ck = HG_CHUNK
    nchunks = seq // ck
    xb = xb_ref[...]
    lb = lb_ref[...]
    q_ref[...] = _silu(_mm(xb, w_ref[0]))
    v_ref[...] = _mm(xb, w_ref[3]).astype(BF16)
    tri_l_b = jnp.where(_tri(ck, True), 1.0, 0.0).astype(BF16)
    tri_u_b = jnp.where(_tri(ck, False), 1.0, 0.0).astype(BF16)
    rr = lax.broadcasted_iota(jnp.int32, (ck, ck), 0)
    cc = lax.broadcasted_iota(jnp.int32, (ck, ck), 1)
    trow = lax.broadcasted_iota(jnp.int32, (ck, dk), 0)
    levels = []
    h = ck // 2
    while h >= 1:
        levels.append(h)
        h //= 2

    def run(backward):
        f = lb + (1.0 - lb) * _sigmoid(_mm(xb, w_ref[2 if backward else 1]))
        k_ref[...] = 1.0 - f
        g_ref[...] = jnp.log(f)
        s_ref[...] = jnp.zeros(s_ref.shape, F32)
        tri_b = tri_u_b if backward else tri_l_b
        far = 0 if backward else ck - 1

        def step(it, carry):
            j = (nchunks - 1 - it) if backward else it
            r0 = pl.multiple_of(j * ck, ck)
            qc = q_ref[pl.ds(r0, ck), :]
            kc = k_ref[pl.ds(r0, ck), :]
            gc = g_ref[pl.ds(r0, ck), :]
            vc = v_ref[pl.ds(r0, ck), :]
            cum = _tri_left(tri_b, gc)
            cumx = cum - gc
            att = jnp.zeros((ck, ck), F32)
            seg_hi = cumx
            seg_lo = cum
            for h in reversed(levels):
                if h > 1:
                    up = pltpu.roll(seg_hi, h // 2, 0)
                    dn = pltpu.roll(seg_hi, ck - h // 2, 0)
                    bit = (trow & (h // 2)) != 0
                    seg_hi = jnp.maximum(seg_hi, jnp.where(bit, up, dn))
                    up = pltpu.roll(seg_lo, h // 2, 0)
                    dn = pltpu.roll(seg_lo, ck - h // 2, 0)
                    seg_lo = jnp.minimum(seg_lo, jnp.where(bit, up, dn))
                hbit = (trow & h) != 0
                q_side = hbit != backward
                qt = jnp.where(q_side, qc * jnp.exp(cum - seg_hi), 0.0).astype(BF16)
                kt = jnp.where(q_side, 0.0, kc * jnp.exp(seg_lo - cum)).astype(BF16)
                same = (rr // (2 * h)) == (cc // (2 * h))
                att = att + jnp.where(same, _mm_nt(qt, kt), 0.0)
            diag = jnp.sum(qc * kc, axis=1, keepdims=True)
            inter = _mm_nt((qc * jnp.exp(cum)).astype(BF16), s_ref[...].astype(BF16))
            oc = _mm(att.astype(BF16), vc) + diag * vc.astype(F32) + inter
            if backward:
                o_ref[pl.ds(r0, ck), :] += oc
            else:
                o_ref[pl.ds(r0, ck), :] = oc
            total = cum[far:far + 1, :]
            kd = (kc * jnp.exp(total - cum)).astype(BF16)
            s_ref[...] = jnp.exp(total) * s_ref[...] + _mm_tn(vc, kd)
            return carry

        lax.fori_loop(0, nchunks, step, 0)

    run(False)
    run(True)
    o = o_ref[...]
    o = o * lax.rsqrt(jnp.mean(o * o, axis=-1, keepdims=True) + LN_EPS) * ng_ref[...]
    y_ref[...] = (o * _silu(_mm(xb, w_ref[4]))).astype(BF16)


def _hgrn2(xb, w_hg, o_idx, lbs, layer, norm_g):
    bsz, seq, d = xb.shape
    heads, dk = w_hg.shape[1], w_hg.shape[-1]
    return pl.pallas_call(
        _hg_kernel,
        grid=(bsz, heads),
        in_specs=[
            pl.BlockSpec((None, seq, d), lambda b, h: (b, 0, 0)),
            pl.BlockSpec((None, None, 5, d, dk), lambda b, h: (o_idx, h, 0, 0, 0)),
            pl.BlockSpec((None, 1, dk), lambda b, h: (layer, 0, h)),
            pl.BlockSpec((None, 1, dk), lambda b, h: (o_idx, 0, h)),
        ],
        out_specs=pl.BlockSpec((None, seq, dk), lambda b, h: (b, 0, h)),
        out_shape=jax.ShapeDtypeStruct((bsz, seq, heads * dk), BF16),
        scratch_shapes=[pltpu.VMEM((seq, dk), F32), pltpu.VMEM((seq, dk), F32), pltpu.VMEM((seq, dk), F32),
                        pltpu.VMEM((seq, dk), BF16), pltpu.VMEM((seq, dk), F32), pltpu.VMEM((dk, dk), F32)],
        compiler_params=_params("parallel", "arbitrary"),
    )(xb, w_hg, lbs, norm_g)


def _ret_kernel(xb_ref, w_ref, cos_ref, sin_ref, ng_ref, y_ref, q_ref, k_ref, v_ref, o_ref, s_ref, *, heads):
    head = pl.program_id(1)
    seq, dk = q_ref.shape
    ck = RET_CHUNK
    nchunks = seq // ck
    half = dk // 2
    xb = xb_ref[...]
    cos = cos_ref[...]
    sin = sin_ref[...]

    def rope(t):
        t1, t2 = t[:, :half], t[:, half:]
        return jnp.concatenate([t1 * cos - t2 * sin, t1 * sin + t2 * cos], axis=-1)

    q_ref[...] = rope(_mm(xb, w_ref[0])).astype(BF16)
    k_ref[...] = rope(_mm(xb, w_ref[1])) * (1.0 / math.sqrt(dk))
    v_ref[...] = _mm(xb, w_ref[2]).astype(BF16)

    hf = jnp.full((1, 1), head, jnp.int32).astype(F32)
    lg_f = jnp.log(1.0 - jnp.exp(-(5.0 + hf) * math.log(2.0)))
    lg_b = jnp.log(1.0 - jnp.exp(-(4.0 + heads - hf) * math.log(2.0)))
    rr = lax.broadcasted_iota(jnp.int32, (ck, ck), 0)
    cc = lax.broadcasted_iota(jnp.int32, (ck, ck), 1)
    diff = (rr - cc).astype(F32)
    intra = (jnp.where(diff >= 0, jnp.exp(jnp.maximum(diff, 0.0) * lg_f), 0.0)
             + jnp.where(diff <= 0, jnp.exp(jnp.maximum(-diff, 0.0) * lg_b), 0.0))
    pos_c = lax.broadcasted_iota(jnp.int32, (ck, 1), 0).astype(F32)

    def sweep(backward):
        lg = lg_b if backward else lg_f
        q_scale = jnp.exp((ck - pos_c if backward else pos_c + 1.0) * lg)
        k_scale = jnp.exp((pos_c if backward else ck - 1.0 - pos_c) * lg)
        chunk_decay = jnp.exp(ck * lg)
        s_ref[...] = jnp.zeros(s_ref.shape, F32)

        def step(it, carry):
            j = (nchunks - 1 - it) if backward else it
            r0 = pl.multiple_of(j * ck, ck)
            qc = q_ref[pl.ds(r0, ck), :]
            kc = k_ref[pl.ds(r0, ck), :]
            vc = v_ref[pl.ds(r0, ck), :]
            inter = q_scale * _mm(qc, s_ref[...].astype(BF16))
            if backward:
                o_ref[pl.ds(r0, ck), :] += inter
            else:
                att = _mm_nt(qc, kc.astype(BF16)) * intra
                o_ref[pl.ds(r0, ck), :] = _mm(att.astype(BF16), vc) + inter
            s_ref[...] = chunk_decay * s_ref[...] + _mm_tn((kc * k_scale).astype(BF16), vc)
            return carry

        lax.fori_loop(0, nchunks, step, 0)

    sweep(False)
    sweep(True)
    o = o_ref[...]
    o = o - jnp.mean(o, axis=-1, keepdims=True)
    o = o * lax.rsqrt(jnp.mean(o * o, axis=-1, keepdims=True) + LN_EPS) * ng_ref[...]
    y_ref[...] = (o * _silu(_mm(xb, w_ref[3]))).astype(BF16)


def _retention(xb, w_ret, o_idx, cos, sin, norm_g):
    bsz, seq, d = xb.shape
    heads, dk = w_ret.shape[1], w_ret.shape[-1]
    return pl.pallas_call(
        functools.partial(_ret_kernel, heads=heads),
        grid=(bsz, heads),
        in_specs=[
            pl.BlockSpec((None, seq, d), lambda b, h: (b, 0, 0)),
            pl.BlockSpec((None, None, 4, d, dk), lambda b, h: (o_idx, h, 0, 0, 0)),
            pl.BlockSpec((seq, dk // 2), lambda b, h: (0, 0)),
            pl.BlockSpec((seq, dk // 2), lambda b, h: (0, 0)),
            pl.BlockSpec((None, 1, dk), lambda b, h: (o_idx, 0, h)),
        ],
        out_specs=pl.BlockSpec((None, seq, dk), lambda b, h: (b, 0, h)),
        out_shape=jax.ShapeDtypeStruct((bsz, seq, heads * dk), BF16),
        scratch_shapes=[pltpu.VMEM((seq, dk), BF16), pltpu.VMEM((seq, dk), F32), pltpu.VMEM((seq, dk), BF16),
                        pltpu.VMEM((seq, dk), F32), pltpu.VMEM((dk, dk), F32)],
        compiler_params=_params("parallel", "arbitrary"),
    )(xb, w_ret, cos, sin, norm_g)


def _block_diag_tiles(w, tile):
    *lead, n, c, _ = w.shape
    per = tile // c
    w = w.reshape(*lead, n // per, per, c, c)
    eye = jnp.eye(per, dtype=w.dtype)
    t = jnp.einsum('...ncd,nm->...ncmd', w, eye)
    return t.reshape(*lead, n // per, tile, tile)


def _rope_tables(seq, dk):
    pos = jnp.arange(seq, dtype=F32)
    inv = ROPE_BASE ** (-jnp.arange(dk // 2, dtype=F32) / (dk // 2))
    ang = pos[:, None] * inv[None, :]
    return jnp.cos(ang), jnp.sin(ang)


def kernel(x_prompt, x_sample, ffn_a_w_in, ffn_a_w_out, ffn_b_w_in, ffn_b_w_out, ln_g, ln_b, ev_w_in, ev_w_out, ml_conv_w, ml_conv_b, ml_wq, ml_wk, ml_wv, ml_w_ig, ml_b_ig, ml_w_fg, ml_b_fg, ml_norm_g, lru_conv_w, lru_conv_b, lru_w_a, lru_b_a, lru_w_x, lru_b_x, lru_lambda, od_w_in, od_w_out, hg_lb_logits, hg_norm_g, ret_norm_g):
    depth, _, d = ln_g.shape
    n_even, n_odd = ev_w_in.shape[0], od_w_in.shape[0]
    alpha = (2.0 * depth) ** 0.25
    ml_w = ml_conv_w.shape[-1]
    dh = ml_w // ML_HEADS
    lru_w = lru_conv_w.shape[-1]
    hg_w = hg_norm_g.shape[-1]
    ret_w = ret_norm_g.shape[-1]
    hg_dk = hg_w // HG_HEADS
    ret_dk = ret_w // RET_HEADS

    fa_in, fa_out = ffn_a_w_in.astype(BF16), ffn_a_w_out.astype(BF16)
    fb_in, fb_out = ffn_b_w_in.astype(BF16), ffn_b_w_out.astype(BF16)
    lng = ln_g.reshape(depth * 3, 1, d)
    lnb = ln_b.reshape(depth * 3, 1, d)
    ev_in, ev_out = ev_w_in.astype(BF16), ev_w_out.astype(BF16)
    od_out = od_w_out.astype(BF16)

    wq_t = _block_diag_tiles(ml_wq, dh).astype(BF16)
    wk_t = (_block_diag_tiles(ml_wk, dh) * (1.0 / math.sqrt(dh))).astype(BF16)
    wv_t = _block_diag_tiles(ml_wv, dh).astype(BF16)
    gw = jnp.stack([ml_w_ig[:, 0], ml_w_ig[:, 1], ml_w_fg[:, 0], ml_w_fg[:, 1]], axis=-1)
    ngate = ML_HEADS * 4
    gw = gw.reshape(n_even, 3, ML_HEADS, dh, ngate).transpose(0, 2, 1, 3, 4)
    gwt = gw.transpose(0, 1, 2, 4, 3).astype(BF16)
    gw = jnp.pad(gw, ((0, 0),) * 4 + ((0, LANES - ngate),)).astype(BF16)
    gb = jnp.stack([ml_b_ig[:, 0], ml_b_ig[:, 1], ml_b_fg[:, 0], ml_b_fg[:, 1]], axis=-1).reshape(n_even, ngate)
    gb_col = jnp.pad(gb, ((0, 0), (0, LANES - ngate))).reshape(n_even, 1, LANES)
    gb_row = gb.reshape(n_even, ngate, 1)
    ml_cb = ml_conv_b.reshape(n_even, 1, ml_w)
    ml_ng = ml_norm_g.reshape(n_even, 1, ml_w)

    wa_t = _block_diag_tiles(lru_w_a, MXU_DIM)
    wx_t = _block_diag_tiles(lru_w_x, MXU_DIM)
    lru_gate = jnp.concatenate([wa_t[:, 0], wx_t[:, 0], wa_t[:, 1], wx_t[:, 1]], axis=-1).astype(BF16)
    groups = lru_w // MXU_DIM
    ba = lru_b_a.reshape(n_even, 2, groups, MXU_DIM)
    bx = lru_b_x.reshape(n_even, 2, groups, MXU_DIM)
    lru_gate_b = jnp.concatenate([ba[:, 0], bx[:, 0], ba[:, 1], bx[:, 1]], axis=-1).reshape(n_even, groups, 1, 4 * MXU_DIM)
    lru_cb = lru_conv_b.reshape(n_even, 1, lru_w)

    w_hg = od_w_in[:, :, :5 * hg_w].reshape(n_odd, d, 5, HG_HEADS, hg_dk).transpose(0, 3, 2, 1, 4).astype(BF16)
    w_ret = od_w_in[:, :, 5 * hg_w:].reshape(n_odd, d, 4, RET_HEADS, ret_dk).transpose(0, 3, 2, 1, 4).astype(BF16)
    hg_ng = hg_norm_g.reshape(n_odd, 1, hg_w)
    ret_ng = ret_norm_g.reshape(n_odd, 1, ret_w)
    lbs = _lower_bounds(hg_lb_logits).reshape(depth, 1, hg_w)

    def trunk(x3):
        bsz, seq, _ = x3.shape
        n = bsz * seq
        cos, sin = _rope_tables(seq, ret_dk)
        x = x3.reshape(n, d)
        for l in range(depth):
            x, xb = _ffn_ln(x, fa_in, fa_out, l, lng, lnb, 0, alpha, True)
            xb3 = xb.reshape(bsz, seq, d)
            if l % 2 == 0:
                e = l // 2
                q, k, v, gc, gr = _ml_prep(xb3, ev_in, e, ml_conv_w, ml_cb, wq_t, wk_t, wv_t, gw, gwt, gb_col, gb_row)
                y1 = _ml_core(q, k, v, gc, gr, xb3, ev_in, e, ml_ng)
                y2 = _lru(xb3, ev_in, e, lru_conv_w, lru_cb, lru_gate, lru_gate_b, lru_lambda)
                w_out, idx = ev_out, e
            else:
                o = l // 2
                y1 = _hgrn2(xb3, w_hg, o, lbs, l, hg_ng)
                y2 = _retention(xb3, w_ret, o, cos, sin, ret_ng)
                w_out, idx = od_out, o
            x = _proj_ln(x, y1.reshape(n, -1), y2.reshape(n, -1), w_out, idx, lng, lnb, l * 3 + 1, alpha)
            (x,) = _ffn_ln(x, fb_in, fb_out, l, lng, lnb, 2, alpha, False)
        return x.reshape(bsz, seq, d)

    return (trunk(x_prompt), trunk(x_sample))
```

```python
import functools
import math

import jax
import jax.numpy as jnp
from jax import lax
from jax.experimental import pallas as pl
from jax.experimental.pallas import tpu as pltpu

F32 = jnp.float32
BF16 = jnp.bfloat16

LN_EPS = 1e-5
CONV_W = 4
CONV_PAD_LEFT = 2
ML_HEADS = 4
ML_QKV_BLOCK = 4
LRU_BLOCKS = 16
LRU_C = 8.0
HG_HEADS = 8
RET_HEADS = 4
ROPE_BASE = 10000.0

LANES = 128
MXU_DIM = 256

TOKEN_TILE = 512
FFN_CHUNK = 1408
ML_CHUNK = MXU_DIM
RET_CHUNK = MXU_DIM
LRU_SEGS = 8
LRU_UNROLL = 8
CHUNK_UNROLL = 2
WIDE_UNROLL = 4
HG_CHUNK = LANES
HG_SAFE_DECAY = 64.0
VMEM_LIMIT = 56 * 1024 * 1024


def _mm(a, b):
    return jnp.dot(a, b, preferred_element_type=F32)


def _mm_nt(a, b):
    return lax.dot_general(a, b, (((1,), (1,)), ((), ())), preferred_element_type=F32)


def _mm_tn(a, b):
    return lax.dot_general(a, b, (((0,), (0,)), ((), ())), preferred_element_type=F32)


def _split3(x):
    x1 = x.astype(BF16)
    r = x - x1.astype(F32)
    x2 = r.astype(BF16)
    r = r - x2.astype(F32)
    return x1, x2, r.astype(BF16)


def _tri_left(tri, x):
    x1, x2, x3 = _split3(x)
    return _mm(tri, x1) + _mm(tri, x2) + _mm(tri, x3)


def _tri_right(x, tri):
    x1, x2, x3 = _split3(x)
    return _mm(x1, tri) + _mm(x2, tri) + _mm(x3, tri)


def _tri(n, lower):
    r = lax.broadcasted_iota(jnp.int32, (n, n), 0)
    c = lax.broadcasted_iota(jnp.int32, (n, n), 1)
    return r >= c if lower else r <= c


def _sigmoid(x):
    return 1.0 / (1.0 + jnp.exp(-x))


def _sigmoid_t(x):
    return 0.5 * jnp.tanh(0.5 * x) + 0.5


def _silu(x):
    return x * _sigmoid(x)


def _log_sigmoid(x):
    return jnp.minimum(x, 0.0) - jnp.log(1.0 + jnp.exp(-jnp.abs(x)))


def _layer_norm(y, g, b):
    mu = jnp.mean(y, axis=-1, keepdims=True)
    d = y - mu
    var = jnp.mean(d * d, axis=-1, keepdims=True)
    return d * lax.rsqrt(var + LN_EPS) * g + b


def _dwconv(u, w, b):
    n = u.shape[0]
    row = lax.broadcasted_iota(jnp.int32, u.shape, 0)
    acc = u * w[CONV_PAD_LEFT:CONV_PAD_LEFT + 1] + b
    for j in range(CONV_W):
        off = j - CONV_PAD_LEFT
        if off == 0:
            continue
        shifted = pltpu.roll(u, (-off) % n, 0)
        valid = (row + off >= 0) & (row + off < n)
        acc = acc + jnp.where(valid, shifted, 0.0) * w[j:j + 1]
    return acc


def _col(x, idx):
    lane = lax.broadcasted_iota(jnp.int32, x.shape, 1)
    return jnp.sum(jnp.where(lane == idx, x, 0.0), axis=1, keepdims=True)


def _row(x, idx):
    sub = lax.broadcasted_iota(jnp.int32, x.shape, 0)
    return jnp.sum(jnp.where(sub == idx, x, 0.0), axis=0, keepdims=True)


def _params(*sem):
    return pltpu.CompilerParams(dimension_semantics=sem, vmem_limit_bytes=VMEM_LIMIT)


def _lbs_kernel(logit_ref, o_ref):
    z = logit_ref[...]
    e = jnp.exp(z - jnp.max(z, axis=0, keepdims=True))
    p = e / jnp.sum(e, axis=0, keepdims=True)
    depth = z.shape[0]
    run = jnp.zeros_like(p[0:1])
    for l in range(1, depth):
        run = run + p[l:l + 1]
        o_ref[l:l + 1, :] = run
    o_ref[0:1, :] = jnp.zeros_like(run)


def _lower_bounds(logits):
    return pl.pallas_call(_lbs_kernel, out_shape=jax.ShapeDtypeStruct(logits.shape, F32))(logits.astype(F32))


def _ffn_kernel(x_ref, win_ref, wout_ref, g_ref, b_ref, *o_refs, d_ff, alpha):
    x = x_ref[...]
    xb = x.astype(BF16)
    acc = jnp.zeros(x.shape, F32)
    for c0 in range(0, d_ff, FFN_CHUNK):
        gate = _mm(xb, win_ref[:, c0:c0 + FFN_CHUNK])
        up = _mm(xb, win_ref[:, d_ff + c0:d_ff + c0 + FFN_CHUNK])
        h = (_silu(gate) * up).astype(BF16)
        acc = acc + _mm(h, wout_ref[c0:c0 + FFN_CHUNK, :])
    y = _layer_norm(alpha * x + 0.5 * acc, g_ref[...], b_ref[...])
    o_refs[0][...] = y
    if len(o_refs) > 1:
        o_refs[1][...] = y.astype(BF16)


def _ffn_ln(x, w_in, w_out, layer, ln_g, ln_b, ln_idx, alpha, with_bf16):
    n, d = x.shape
    d_ff = w_out.shape[1]
    tm = min(TOKEN_TILE, n)
    assert n % tm == 0 and d_ff % FFN_CHUNK == 0
    out_shape = [jax.ShapeDtypeStruct((n, d), F32)]
    out_specs = [pl.BlockSpec((tm, d), lambda i: (i, 0))]
    if with_bf16:
        out_shape.append(jax.ShapeDtypeStruct((n, d), BF16))
        out_specs.append(pl.BlockSpec((tm, d), lambda i: (i, 0)))
    const = dict(pipeline_mode=pl.Buffered(1))
    return pl.pallas_call(
        functools.partial(_ffn_kernel, d_ff=d_ff, alpha=alpha),
        grid=(n // tm,),
        in_specs=[
            pl.BlockSpec((tm, d), lambda i: (i, 0)),
            pl.BlockSpec((None, d, 2 * d_ff), lambda i: (layer, 0, 0), **const),
            pl.BlockSpec((None, d_ff, d), lambda i: (layer, 0, 0), **const),
            pl.BlockSpec((None, 1, d), lambda i: (layer * 3 + ln_idx, 0, 0)),
            pl.BlockSpec((None, 1, d), lambda i: (layer * 3 + ln_idx, 0, 0)),
        ],
        out_specs=out_specs,
        out_shape=out_shape,
        compiler_params=_params("parallel"),
        name="ffn_ln",
    )(x, w_in, w_out, ln_g, ln_b)


def _proj_ln_kernel(x_ref, y1_ref, y2_ref, w_ref, g_ref, b_ref, o_ref, *, alpha):
    w1 = y1_ref.shape[1]
    acc = _mm(y1_ref[...], w_ref[0:w1, :]) + _mm(y2_ref[...], w_ref[w1:, :])
    o_ref[...] = _layer_norm(alpha * x_ref[...] + acc, g_ref[...], b_ref[...])


def _proj_ln(x, y1, y2, w_out, idx, ln_g, ln_b, ln_row, alpha):
    n, d = x.shape
    w1, w2 = y1.shape[1], y2.shape[1]
    tm = min(TOKEN_TILE, n)
    return pl.pallas_call(
        functools.partial(_proj_ln_kernel, alpha=alpha),
        grid=(n // tm,),
        in_specs=[
            pl.BlockSpec((tm, d), lambda i: (i, 0)),
            pl.BlockSpec((tm, w1), lambda i: (i, 0)),
            pl.BlockSpec((tm, w2), lambda i: (i, 0)),
            pl.BlockSpec((None, w1 + w2, d), lambda i: (idx, 0, 0)),
            pl.BlockSpec((None, 1, d), lambda i: (ln_row, 0, 0)),
            pl.BlockSpec((None, 1, d), lambda i: (ln_row, 0, 0)),
        ],
        out_specs=pl.BlockSpec((tm, d), lambda i: (i, 0)),
        out_shape=jax.ShapeDtypeStruct((n, d), F32),
        compiler_params=_params("parallel"),
        name="proj_ln",
    )(x, y1, y2, w_out, ln_g, ln_b)


def _ml_prep_kernel(xb_ref, wu_ref, cw_ref, cb_ref, wq_ref, wk_ref, wv_ref, gw_ref, bcol_ref,
                    q_ref, k_ref, v_ref, gc_ref):
    u = _mm(xb_ref[...], wu_ref[...])
    c = _silu(_dwconv(u, cw_ref[...], cb_ref[...])).astype(BF16)
    qb = _mm(c, wq_ref[...]).astype(BF16)
    kb = _mm(c, wk_ref[...]).astype(BF16)
    vb = _mm(u.astype(BF16), wv_ref[...]).astype(BF16)
    q_ref[...] = qb
    k_ref[...] = kb
    v_ref[...] = vb

    @pl.when(pl.program_id(1) == 0)
    def _():
        gc_ref[...] = jnp.broadcast_to(bcol_ref[...], gc_ref.shape)

    gc_ref[...] += _mm(qb, gw_ref[0]) + _mm(kb, gw_ref[1]) + _mm(vb, gw_ref[2])


def _ml_prep(xb, w_in, e, cw, cb, wq, wk, wv, gw, bcol):
    bsz, seq, d = xb.shape
    dh = wq.shape[-1]
    heads = ML_HEADS
    qkv = jax.ShapeDtypeStruct((bsz, seq, heads * dh), BF16)
    head_blk = pl.BlockSpec((None, seq, dh), lambda b, h: (b, 0, h))
    return pl.pallas_call(
        _ml_prep_kernel,
        grid=(bsz, heads),
        in_specs=[
            pl.BlockSpec((None, seq, d), lambda b, h: (b, 0, 0)),
            pl.BlockSpec((None, d, dh), lambda b, h: (e, 0, h)),
            pl.BlockSpec((None, CONV_W, dh), lambda b, h: (e, 0, h)),
            pl.BlockSpec((None, 1, dh), lambda b, h: (e, 0, h)),
            pl.BlockSpec((None, None, dh, dh), lambda b, h: (e, h, 0, 0)),
            pl.BlockSpec((None, None, dh, dh), lambda b, h: (e, h, 0, 0)),
            pl.BlockSpec((None, None, dh, dh), lambda b, h: (e, h, 0, 0)),
            pl.BlockSpec((None, None, 3, dh, LANES), lambda b, h: (e, h, 0, 0, 0)),
            pl.BlockSpec((None, 1, LANES), lambda b, h: (e, 0, 0)),
        ],
        out_specs=[head_blk, head_blk, head_blk,
                   pl.BlockSpec((None, seq, LANES), lambda b, h: (b, 0, 0))],
        out_shape=[qkv, qkv, qkv, jax.ShapeDtypeStruct((bsz, seq, LANES), F32)],
        compiler_params=_params("parallel", "arbitrary"),
        name="mlstm_prep",
    )(xb, w_in, cw, cb, wq, wk, wv, gw, bcol)


def _ml_gates_kernel(gc_ref, cf_ref, cb_ref):
    ck = ML_CHUNK
    tri_l_b = jnp.where(_tri(ck, True), 1.0, 0.0).astype(BF16)
    tri_u_b = jnp.where(_tri(ck, False), 1.0, 0.0).astype(BF16)

    def step(c, carry):
        rows = pl.ds(pl.multiple_of(c * ck, ck), ck)
        lf = _log_sigmoid(gc_ref[rows, :])
        cf_ref[rows, :] = _tri_left(tri_l_b, lf)
        cb_ref[rows, :] = _tri_left(tri_u_b, lf)
        return carry

    lax.fori_loop(0, gc_ref.shape[0] // ck, step, 0)


def _ml_gates(gc):
    bsz, seq, w = gc.shape
    blk = pl.BlockSpec((None, seq, w), lambda b: (b, 0, 0))
    out = jax.ShapeDtypeStruct(gc.shape, F32)
    return pl.pallas_call(
        _ml_gates_kernel, grid=(bsz,), in_specs=[blk], out_specs=[blk, blk], out_shape=[out, out],
        compiler_params=_params("parallel"), name="mlstm_gates",
    )(gc)


def _ml_core_kernel(q_ref, k_ref, v_ref, gc_ref, cf_ref, cb_ref, gr_ref, cfr_ref, cbr_ref, xb_ref, wz_ref, ng_ref,
                    y_ref, h_ref, col_ref, c_ref, n_ref):
    head = pl.program_id(1)
    seq, dh = q_ref.shape
    ck = ML_CHUNK
    nchunks = seq // ck
    masks = (_tri(ck, True), _tri(ck, False))
    wide = lambda x: jnp.concatenate([x] * (dh // LANES), axis=1)
    cum_src = (cf_ref, cb_ref)
    cum_row_src = (cfr_ref, cbr_ref)
    for dirn in range(2):
        col_ref[dirn] = jnp.broadcast_to(_col(cum_src[dirn][...], head * 4 + 2 + dirn), (seq, LANES))
        col_ref[2 + dirn] = jnp.broadcast_to(_col(gc_ref[...], head * 4 + dirn), (seq, LANES))
    c_ref[...] = jnp.zeros(c_ref.shape, F32)
    n_ref[...] = jnp.zeros(n_ref.shape, F32)
    h_ref[...] = jnp.zeros(h_ref.shape, F32)

    def step(it, m_state):
        m_out = []
        for dirn in range(2):
            backward = dirn == 1
            m_prev = m_state[dirn]
            j = (nchunks - 1 - it) if backward else it
            rows = pl.ds(pl.multiple_of(j * ck, ck), ck)
            qc = q_ref[rows, :]
            kc = k_ref[rows, :]
            vc = v_ref[rows, :]
            cum_col = col_ref[dirn, rows, :]
            i_col = col_ref[2 + dirn, rows, :]
            cum_row = _row(cum_row_src[dirn][j], head * 4 + 2 + dirn)
            i_row = _row(gr_ref[j], head * 4 + dirn)
            far = 0 if backward else ck - 1
            total = cum_row[:, far:far + 1]

            d_log = jnp.where(masks[dirn], wide(cum_col) + (i_row - cum_row), -jnp.inf)
            inter_log = cum_col + m_prev
            m_t = jnp.maximum(inter_log, jnp.max(d_log, axis=1, keepdims=True))
            p = _mm_nt(qc, kc) * jnp.exp(d_log - wide(m_t))
            w_inter = jnp.exp(inter_log - m_t)
            num = _mm(p.astype(BF16), vc) + wide(w_inter) * _mm(qc, c_ref[dirn].astype(BF16))
            qn = jnp.sum(qc.astype(F32) * n_ref[dirn], axis=1, keepdims=True)
            den = jnp.sum(p, axis=1, keepdims=True) + w_inter * qn
            h_ref[rows, :] += num / wide(jnp.maximum(jnp.abs(den), jnp.exp(-m_t)))

            g_col = total - cum_col + i_col
            g_row = total - cum_row + i_row
            m_new = jnp.maximum(total + m_prev, jnp.max(g_row, axis=1, keepdims=True))
            kw = kc.astype(F32) * wide(jnp.exp(g_col - m_new))
            decay = jnp.exp(total + m_prev - m_new)
            c_ref[dirn] = decay * c_ref[dirn] + _mm_tn(kw.astype(BF16), vc)
            n_ref[dirn] = decay * n_ref[dirn] + jnp.sum(kw, axis=0, keepdims=True)
            m_out.append(m_new)
        return tuple(m_out)

    lax.fori_loop(0, nchunks, step, (jnp.zeros((1, 1), F32), jnp.zeros((1, 1), F32)), unroll=CHUNK_UNROLL)
    h = h_ref[...]
    h = h - jnp.mean(h, axis=-1, keepdims=True)
    h = h * lax.rsqrt(jnp.mean(h * h, axis=-1, keepdims=True) + LN_EPS) * ng_ref[...]
    z = _mm(xb_ref[...], wz_ref[...])
    y_ref[...] = (_sigmoid_t(z) * h).astype(BF16)


def _ml_core(q, k, v, gc, cf, cb, xb, w_in, e, norm_g):
    bsz, seq, width = q.shape
    d = xb.shape[-1]
    heads = ML_HEADS
    dh = width // heads
    nchunks = seq // ML_CHUNK
    ng = heads * 4

    def rows(a):
        return a[:, :, :ng].reshape(bsz, nchunks, ML_CHUNK, ng).transpose(0, 1, 3, 2)

    head_blk = pl.BlockSpec((None, seq, dh), lambda b, h: (b, 0, h))
    col_blk = pl.BlockSpec((None, seq, LANES), lambda b, h: (b, 0, 0))
    row_blk = pl.BlockSpec((None, nchunks, ng, ML_CHUNK), lambda b, h: (b, 0, 0, 0))
    return pl.pallas_call(
        _ml_core_kernel,
        grid=(bsz, heads),
        in_specs=[
            head_blk, head_blk, head_blk,
            col_blk, col_blk, col_blk,
            row_blk, row_blk, row_blk,
            pl.BlockSpec((None, seq, d), lambda b, h: (b, 0, 0)),
            pl.BlockSpec((None, d, dh), lambda b, h: (e, 0, heads + h)),
            pl.BlockSpec((None, 1, dh), lambda b, h: (e, 0, h)),
        ],
        out_specs=head_blk,
        out_shape=jax.ShapeDtypeStruct((bsz, seq, width), BF16),
        scratch_shapes=[pltpu.VMEM((seq, dh), F32), pltpu.VMEM((4, seq, LANES), F32),
                        pltpu.VMEM((2, dh, dh), F32), pltpu.VMEM((2, 1, dh), F32)],
        compiler_params=_params("parallel", "arbitrary"),
        name="mlstm_core",
    )(q, k, v, gc, cf, cb, rows(gc), rows(cf), rows(cb), xb, w_in, norm_g)


def _dwconv_interleaved(u, w, b, nseg):
    n = u.shape[0]
    row = lax.broadcasted_iota(jnp.int32, u.shape, 0)
    seg = row & (nseg - 1)
    acc = u * w[CONV_PAD_LEFT:CONV_PAD_LEFT + 1] + b
    for j in range(CONV_W):
        off = j - CONV_PAD_LEFT
        if off == 0:
            continue
        if off < 0:
            edge = row < nseg * -off
            shifted = jnp.where(edge, pltpu.roll(u, nseg * -off + 1, 0), pltpu.roll(u, nseg * -off, 0))
            valid = jnp.logical_not(jnp.logical_and(edge, seg == 0))
        else:
            edge = row >= n - nseg * off
            shifted = jnp.where(edge, pltpu.roll(u, n - nseg * off - 1, 0), pltpu.roll(u, n - nseg * off, 0))
            valid = jnp.logical_not(jnp.logical_and(edge, seg == nseg - 1))
        acc = acc + jnp.where(valid, shifted, 0.0) * w[j:j + 1]
    return acc


def _lru_kernel(xb_ref, wu_ref, wg_ref, cw_ref, cb_ref, wgate_ref, bgate_ref, lam_ref, y_ref, a_ref, b_ref):
    _, n, w = a_ref.shape
    nseg = LRU_SEGS
    steps = n // nseg
    xb = xb_ref[...]
    u = _mm(xb, wu_ref[...])
    x = _dwconv_interleaved(u, cw_ref[...], cb_ref[...], nseg)
    pre = _mm(x.astype(BF16), wgate_ref[...]) + bgate_ref[...]
    lam = lam_ref[...]
    sp = jnp.maximum(-lam, 0.0) + jnp.log(1.0 + jnp.exp(-jnp.abs(lam)))
    for dirn in range(2):
        r = _sigmoid_t(pre[:, (2 * dirn) * w:(2 * dirn + 1) * w])
        i = _sigmoid_t(pre[:, (2 * dirn + 1) * w:(2 * dirn + 2) * w])
        a = jnp.exp(-LRU_C * r * sp[dirn:dirn + 1])
        a_ref[dirn] = a
        b_ref[dirn] = jnp.sqrt(1.0 - a * a) * i * x

    def step(it, carry):
        out = []
        for dirn in range(2):
            h, p = carry[2 * dirn], carry[2 * dirn + 1]
            j = (steps - 1 - it) if dirn else it
            rows = pl.ds(pl.multiple_of(j * nseg, nseg), nseg)
            a = a_ref[dirn, rows, :]
            h = a * h + b_ref[dirn, rows, :]
            p = p * a
            b_ref[dirn, rows, :] = h
            a_ref[dirn, rows, :] = p
            out += [h, p]
        return tuple(out)

    zeros = jnp.zeros((nseg, w), F32)
    ones = jnp.ones((nseg, w), F32)
    h_f, p_f, h_b, p_b = lax.fori_loop(0, steps, step, (zeros, ones, zeros, ones), unroll=LRU_UNROLL)
    srow = lax.broadcasted_iota(jnp.int32, (nseg, w), 0)
    c_f, c_b = zeros, zeros
    for _ in range(nseg - 1):
        c_f = jnp.where(srow >= 1, pltpu.roll(h_f + p_f * c_f, 1, 0), 0.0)
        c_b = jnp.where(srow <= nseg - 2, pltpu.roll(h_b + p_b * c_b, nseg - 1, 0), 0.0)
    split = lambda ref, d: ref[d].reshape(steps, nseg, w)
    h = split(b_ref, 0) + split(a_ref, 0) * c_f[None] + split(b_ref, 1) + split(a_ref, 1) * c_b[None]
    g = _mm(xb, wg_ref[...])
    y_ref[...] = (h.reshape(n, w) * jax.nn.gelu(g, approximate=True)).astype(BF16)


def _lru(xb, w_in, e, cw, cb, wgate, bgate, lam):
    bsz, seq, d = xb.shape
    groups, w = wgate.shape[1], wgate.shape[2]
    ublk = 2 * ML_HEADS
    return pl.pallas_call(
        _lru_kernel,
        grid=(bsz, groups),
        in_specs=[
            pl.BlockSpec((None, seq, d), lambda b, j: (b, 0, 0)),
            pl.BlockSpec((None, d, w), lambda b, j: (e, 0, ublk + j)),
            pl.BlockSpec((None, d, w), lambda b, j: (e, 0, ublk + groups + j)),
            pl.BlockSpec((None, CONV_W, w), lambda b, j: (e, 0, j)),
            pl.BlockSpec((None, 1, w), lambda b, j: (e, 0, j)),
            pl.BlockSpec((None, None, w, 4 * w), lambda b, j: (e, j, 0, 0)),
            pl.BlockSpec((None, None, 1, 4 * w), lambda b, j: (e, j, 0, 0)),
            pl.BlockSpec((None, 2, w), lambda b, j: (e, 0, j)),
        ],
        out_specs=pl.BlockSpec((None, seq, w), lambda b, j: (b, 0, j)),
        out_shape=jax.ShapeDtypeStruct((bsz, seq, groups * w), BF16),
        scratch_shapes=[pltpu.VMEM((2, seq, w), F32), pltpu.VMEM((2, seq, w), F32)],
        compiler_params=_params("parallel", "arbitrary"),
        name="rglru",
    )(xb, w_in, w_in, cw, cb, wgate, bgate, lam)


def _interleave(x, nseg):
    bsz, seq, w = x.shape
    return x.reshape(bsz, nseg, seq // nseg, w).transpose(0, 2, 1, 3).reshape(bsz, seq, w)


def _deinterleave(x, nseg):
    bsz, seq, w = x.shape
    return x.reshape(bsz, seq // nseg, nseg, w).transpose(0, 2, 1, 3).reshape(bsz, seq, w)


def _seg_bounds(cum, h, backward):
    n, w = cum.shape
    zero = jnp.zeros((1, w), F32)
    his, los = [], []
    for s0 in range(0, n, h):
        e0 = s0 + h - 1
        if backward:
            hi = cum[e0 + 1:e0 + 2] if e0 + 1 < n else zero
            lo = cum[s0:s0 + 1]
        else:
            hi = cum[s0 - 1:s0] if s0 > 0 else zero
            lo = cum[e0:e0 + 1]
        his.append(jnp.broadcast_to(hi, (h, w)))
        los.append(jnp.broadcast_to(lo, (h, w)))
    if len(his) == 1:
        return his[0], los[0]
    return jnp.concatenate(his, axis=0), jnp.concatenate(los, axis=0)


def _pair_rhs(x, dk):
    lane = lax.broadcasted_iota(jnp.int32, x.shape, 1)
    z = jnp.zeros_like(x)
    return jnp.concatenate([jnp.where(lane < dk, x, z), jnp.where(lane >= dk, x, z)], axis=0)


def _hg_variant(d32, d64, d128):
    use128 = d128 < HG_SAFE_DECAY
    use64 = jnp.logical_and(jnp.logical_not(use128), d64 < HG_SAFE_DECAY)
    use32 = jnp.logical_and(jnp.logical_not(jnp.logical_or(use128, use64)), d32 < HG_SAFE_DECAY)
    use1 = jnp.logical_not(jnp.logical_or(jnp.logical_or(use128, use64), use32))
    return {128: use128, 64: use64, 32: use32, 1: use1}


def _hg_kernel(xb_ref, w_ref, lb_ref, ng_ref, y_ref, q_ref, k_ref, g_ref, cum_ref, v_ref, o_ref, s_ref):
    seq, dk2 = q_ref.shape
    dk = dk2 // 2
    ck = HG_CHUNK
    nchunks = seq // ck
    xb = xb_ref[...]
    lb = lb_ref[...]
    q_ref[...] = _silu(_mm(xb, w_ref[0]))
    v_ref[...] = _mm(xb, w_ref[3]).astype(BF16)
    for dirn in range(2):
        f = lb + (1.0 - lb) * _sigmoid(_mm(xb, w_ref[1 + dirn]))
        k_ref[dirn] = 1.0 - f
        g_ref[dirn] = jnp.log(f)
    tri = (_tri(ck, True), _tri(ck, False))
    tri_b = tuple(jnp.where(m, 1.0, 0.0).astype(BF16) for m in tri)
    rr = lax.broadcasted_iota(jnp.int32, (ck, ck), 0)
    cc = lax.broadcasted_iota(jnp.int32, (ck, ck), 1)
    blk_xor = rr ^ cc
    trow = lax.broadcasted_iota(jnp.int32, (ck, dk2), 0)
    tlane = lax.broadcasted_iota(jnp.int32, (ck, dk2), 1)
    srow = lax.broadcasted_iota(jnp.int32, (dk2, dk2), 0)
    scol = lax.broadcasted_iota(jnp.int32, (dk2, dk2), 1)
    state_diag = (srow < dk) == (scol < dk)

    def cum_step(c, worst):
        r0 = pl.multiple_of(c * ck, ck)
        w32, w64, w128 = worst
        for dirn in range(2):
            cum = _tri_left(tri_b[dirn], g_ref[dirn, pl.ds(r0, ck), :])
            cum_ref[dirn, pl.ds(r0, ck), :] = cum
            ends = [ck - (i + 1) * 32 if dirn else (i + 1) * 32 - 1 for i in range(ck // 32)]
            c0, c1, c2, c3 = [cum[e:e + 1, :] for e in ends]
            w32 = jnp.maximum(w32, jnp.maximum(jnp.maximum(-c0, c0 - c1), jnp.maximum(c1 - c2, c2 - c3)))
            w64 = jnp.maximum(w64, jnp.maximum(-c1, c1 - c3))
            w128 = jnp.maximum(w128, -c3)
        return w32, w64, w128

    zero_row = jnp.zeros((1, dk2), F32)
    w32, w64, w128 = lax.fori_loop(0, nchunks, cum_step, (zero_row, zero_row, zero_row), unroll=WIDE_UNROLL)
    variant = _hg_variant(jnp.max(w32), jnp.max(w64), jnp.max(w128))

    def run(blk):
        s_ref[...] = jnp.zeros(s_ref.shape, F32)
        o_ref[...] = jnp.zeros(o_ref.shape, F32)
        big_levels = [h for h in (64, 32, 16, 8) if blk <= h < ck]
        small_levels = [h for h in (1, 2, 4) if h >= blk]
        pair = lambda m: jnp.concatenate([m, m], axis=1)
        in_blk = [pair(jnp.logical_and(tri[d], blk_xor < blk)) for d in range(2)] if blk > 1 else None
        same2h = {h: pair(blk_xor < 2 * h) for h in big_levels + small_levels}

        def add(att, part):
            return part if att is None else att + part

        def step(it, carry):
            for dirn in range(2):
                backward = dirn == 1
                j = (nchunks - 1 - it) if backward else it
                r0 = pl.multiple_of(j * ck, ck)
                qc = q_ref[pl.ds(r0, ck), :]
                kc = k_ref[dirn, pl.ds(r0, ck), :]
                cum = cum_ref[dirn, pl.ds(r0, ck), :]
                vc = v_ref[pl.ds(r0, ck), :]
                q_in = (qc * jnp.exp(cum)).astype(BF16)
                att = None
                if blk == ck:
                    kt = (kc * jnp.exp(-cum)).astype(BF16)
                    att = jnp.where(in_blk[dirn], _mm_nt(q_in, _pair_rhs(kt, dk)), 0.0)
                elif blk > 1:
                    ref, _ = _seg_bounds(cum, blk, backward)
                    qt = (qc * jnp.exp(cum - ref)).astype(BF16)
                    kt = (kc * jnp.exp(ref - cum)).astype(BF16)
                    att = jnp.where(in_blk[dirn], _mm_nt(qt, _pair_rhs(kt, dk)), 0.0)
                for h in big_levels:
                    seg_hi, seg_lo = _seg_bounds(cum, h, backward)
                    q_side = ((trow & h) != 0) != backward
                    qt = jnp.where(q_side, qc * jnp.exp(cum - seg_hi), 0.0).astype(BF16)
                    kt = jnp.where(q_side, 0.0, kc * jnp.exp(seg_lo - cum)).astype(BF16)
                    att = add(att, jnp.where(same2h[h], _mm_nt(qt, _pair_rhs(kt, dk)), 0.0))
                if small_levels:
                    seg_hi = cum - g_ref[dirn, pl.ds(r0, ck), :]
                    seg_lo = cum
                    for h in small_levels:
                        if h > 1:
                            bit = (trow & (h // 2)) != 0
                            seg_hi = jnp.maximum(seg_hi, jnp.where(bit, pltpu.roll(seg_hi, h // 2, 0),
                                                                   pltpu.roll(seg_hi, ck - h // 2, 0)))
                            seg_lo = jnp.minimum(seg_lo, jnp.where(bit, pltpu.roll(seg_lo, h // 2, 0),
                                                                   pltpu.roll(seg_lo, ck - h // 2, 0)))
                        q_side = ((trow & h) != 0) != backward
                        qt = jnp.where(q_side, qc * jnp.exp(cum - seg_hi), 0.0).astype(BF16)
                        kt = jnp.where(q_side, 0.0, kc * jnp.exp(seg_lo - cum)).astype(BF16)
                        att = add(att, jnp.where(same2h[h], _mm_nt(qt, _pair_rhs(kt, dk)), 0.0))
                oc = _mm(att.astype(BF16), _pair_rhs(vc, dk)) + _mm_nt(q_in, s_ref[dirn].astype(BF16))
                if blk == 1:
                    prod = qc * kc
                    da = jnp.sum(prod[:, :dk], axis=1, keepdims=True)
                    db = jnp.sum(prod[:, dk:], axis=1, keepdims=True)
                    oc = oc + jnp.where(tlane < dk, da, db) * vc.astype(F32)
                o_ref[pl.ds(r0, ck), :] += oc
                far = 0 if backward else ck - 1
                total = cum[far:far + 1, :]
                kd = (kc * jnp.exp(total - cum)).astype(BF16)
                s_ref[dirn] = jnp.where(state_diag, jnp.exp(total) * s_ref[dirn] + _mm_tn(vc, kd), 0.0)
            return carry

        lax.fori_loop(0, nchunks, step, 0, unroll={128: WIDE_UNROLL, 1: 1}.get(blk, CHUNK_UNROLL))

    for blk, use in variant.items():
        pl.when(use)(functools.partial(run, blk))

    o = o_ref[...]
    ng = ng_ref[...]
    halves = []
    for lo in (0, dk):
        oh = o[:, lo:lo + dk]
        halves.append(oh * lax.rsqrt(jnp.mean(oh * oh, axis=-1, keepdims=True) + LN_EPS) * ng[:, lo:lo + dk])
    y_ref[...] = (jnp.concatenate(halves, axis=1) * _silu(_mm(xb, w_ref[4]))).astype(BF16)


def _hgrn2(xb, w_hg, o_idx, lbs, layer, norm_g):
    bsz, seq, d = xb.shape
    pairs, dk2 = w_hg.shape[1], w_hg.shape[-1]
    wide = lambda: pltpu.VMEM((seq, dk2), F32)
    both = lambda: pltpu.VMEM((2, seq, dk2), F32)
    return pl.pallas_call(
        _hg_kernel,
        grid=(bsz, pairs),
        in_specs=[
            pl.BlockSpec((None, seq, d), lambda b, h: (b, 0, 0)),
            pl.BlockSpec((None, None, 5, d, dk2), lambda b, h: (o_idx, h, 0, 0, 0)),
            pl.BlockSpec((None, 1, dk2), lambda b, h: (layer, 0, h)),
            pl.BlockSpec((None, 1, dk2), lambda b, h: (o_idx, 0, h)),
        ],
        out_specs=pl.BlockSpec((None, seq, dk2), lambda b, h: (b, 0, h)),
        out_shape=jax.ShapeDtypeStruct((bsz, seq, pairs * dk2), BF16),
        scratch_shapes=[wide(), both(), both(), both(), pltpu.VMEM((seq, dk2), BF16), wide(),
                        pltpu.VMEM((2, dk2, dk2), F32)],
        compiler_params=_params("parallel", "arbitrary"),
        name="hgrn2",
    )(xb, w_hg, lbs, norm_g)


def _ret_kernel(xb_ref, w_ref, cos_ref, sin_ref, ng_ref, y_ref, q_ref, k_ref, v_ref, o_ref, s_ref, *, heads):
    head = pl.program_id(1)
    seq, dk = q_ref.shape
    ck = RET_CHUNK
    nchunks = seq // ck
    half = dk // 2
    xb = xb_ref[...]
    cos = cos_ref[...]
    sin = sin_ref[...]

    def rope(t):
        t1, t2 = t[:, :half], t[:, half:]
        return jnp.concatenate([t1 * cos - t2 * sin, t1 * sin + t2 * cos], axis=-1)

    q_ref[...] = rope(_mm(xb, w_ref[0])).astype(BF16)
    k_ref[...] = rope(_mm(xb, w_ref[1])) * (1.0 / math.sqrt(dk))
    v_ref[...] = _mm(xb, w_ref[2]).astype(BF16)

    hf = jnp.full((1, 1), head, jnp.int32).astype(F32)
    lg_f = jnp.log(1.0 - jnp.exp(-(5.0 + hf) * math.log(2.0)))
    lg_b = jnp.log(1.0 - jnp.exp(-(4.0 + heads - hf) * math.log(2.0)))
    rr = lax.broadcasted_iota(jnp.int32, (ck, ck), 0)
    cc = lax.broadcasted_iota(jnp.int32, (ck, ck), 1)
    diff = (rr - cc).astype(F32)
    intra = (jnp.where(diff >= 0, jnp.exp(jnp.maximum(diff, 0.0) * lg_f), 0.0)
             + jnp.where(diff <= 0, jnp.exp(jnp.maximum(-diff, 0.0) * lg_b), 0.0))
    pos_c = lax.broadcasted_iota(jnp.int32, (ck, 1), 0).astype(F32)

    lgs = (lg_f, lg_b)
    q_scale = (jnp.exp((pos_c + 1.0) * lg_f), jnp.exp((ck - pos_c) * lg_b))
    k_scale = (jnp.exp((ck - 1.0 - pos_c) * lg_f), jnp.exp(pos_c * lg_b))
    s_ref[...] = jnp.zeros(s_ref.shape, F32)
    o_ref[...] = jnp.zeros(o_ref.shape, F32)

    def step(it, carry):
        for dirn in range(2):
            j = (nchunks - 1 - it) if dirn else it
            rows = pl.ds(pl.multiple_of(j * ck, ck), ck)
            qc = q_ref[rows, :]
            kc = k_ref[rows, :]
            vc = v_ref[rows, :]
            oc = q_scale[dirn] * _mm(qc, s_ref[dirn].astype(BF16))
            if dirn == 0:
                att = _mm_nt(qc, kc.astype(BF16)) * intra
                oc = oc + _mm(att.astype(BF16), vc)
            o_ref[rows, :] += oc
            s_ref[dirn] = jnp.exp(ck * lgs[dirn]) * s_ref[dirn] + _mm_tn((kc * k_scale[dirn]).astype(BF16), vc)
        return carry

    lax.fori_loop(0, nchunks, step, 0, unroll=WIDE_UNROLL)
    o = o_ref[...]
    o = o - jnp.mean(o, axis=-1, keepdims=True)
    o = o * lax.rsqrt(jnp.mean(o * o, axis=-1, keepdims=True) + LN_EPS) * ng_ref[...]
    y_ref[...] = (o * _silu(_mm(xb, w_ref[3]))).astype(BF16)


def _retention(xb, w_ret, o_idx, cos, sin, norm_g):
    bsz, seq, d = xb.shape
    heads, dk = w_ret.shape[1], w_ret.shape[-1]
    return pl.pallas_call(
        functools.partial(_ret_kernel, heads=heads),
        grid=(bsz, heads),
        in_specs=[
            pl.BlockSpec((None, seq, d), lambda b, h: (b, 0, 0)),
            pl.BlockSpec((None, None, 4, d, dk), lambda b, h: (o_idx, h, 0, 0, 0)),
            pl.BlockSpec((seq, dk // 2), lambda b, h: (0, 0)),
            pl.BlockSpec((seq, dk // 2), lambda b, h: (0, 0)),
            pl.BlockSpec((None, 1, dk), lambda b, h: (o_idx, 0, h)),
        ],
        out_specs=pl.BlockSpec((None, seq, dk), lambda b, h: (b, 0, h)),
        out_shape=jax.ShapeDtypeStruct((bsz, seq, heads * dk), BF16),
        scratch_shapes=[pltpu.VMEM((seq, dk), BF16), pltpu.VMEM((seq, dk), F32), pltpu.VMEM((seq, dk), BF16),
                        pltpu.VMEM((seq, dk), F32), pltpu.VMEM((2, dk, dk), F32)],
        compiler_params=_params("parallel", "arbitrary"),
        name="retention",
    )(xb, w_ret, cos, sin, norm_g)


def _block_diag_tiles(w, tile):
    *lead, n, c, _ = w.shape
    per = tile // c
    w = w.reshape(*lead, n // per, per, c, c)
    eye = jnp.eye(per, dtype=w.dtype)
    t = jnp.einsum('...ncd,nm->...ncmd', w, eye)
    return t.reshape(*lead, n // per, tile, tile)


def _rope_tables(seq, dk):
    pos = jnp.arange(seq, dtype=F32)
    inv = ROPE_BASE ** (-jnp.arange(dk // 2, dtype=F32) / (dk // 2))
    ang = pos[:, None] * inv[None, :]
    return jnp.cos(ang), jnp.sin(ang)


def kernel(x_prompt, x_sample, ffn_a_w_in, ffn_a_w_out, ffn_b_w_in, ffn_b_w_out, ln_g, ln_b, ev_w_in, ev_w_out, ml_conv_w, ml_conv_b, ml_wq, ml_wk, ml_wv, ml_w_ig, ml_b_ig, ml_w_fg, ml_b_fg, ml_norm_g, lru_conv_w, lru_conv_b, lru_w_a, lru_b_a, lru_w_x, lru_b_x, lru_lambda, od_w_in, od_w_out, hg_lb_logits, hg_norm_g, ret_norm_g):
    depth, _, d = ln_g.shape
    n_even, n_odd = ev_w_in.shape[0], od_w_in.shape[0]
    alpha = (2.0 * depth) ** 0.25
    ml_w = ml_conv_w.shape[-1]
    dh = ml_w // ML_HEADS
    lru_w = lru_conv_w.shape[-1]
    hg_w = hg_norm_g.shape[-1]
    ret_w = ret_norm_g.shape[-1]
    hg_dk = hg_w // HG_HEADS
    ret_dk = ret_w // RET_HEADS

    fa_in, fa_out = ffn_a_w_in.astype(BF16), ffn_a_w_out.astype(BF16)
    fb_in, fb_out = ffn_b_w_in.astype(BF16), ffn_b_w_out.astype(BF16)
    lng = ln_g.reshape(depth * 3, 1, d)
    lnb = ln_b.reshape(depth * 3, 1, d)
    ev_in, ev_out = ev_w_in.astype(BF16), ev_w_out.astype(BF16)
    od_out = od_w_out.astype(BF16)

    wq_t = _block_diag_tiles(ml_wq, dh).astype(BF16)
    wk_t = (_block_diag_tiles(ml_wk, dh) * (1.0 / math.sqrt(dh))).astype(BF16)
    wv_t = _block_diag_tiles(ml_wv, dh).astype(BF16)
    gw = jnp.stack([ml_w_ig[:, 0], ml_w_ig[:, 1], ml_w_fg[:, 0], ml_w_fg[:, 1]], axis=-1)
    ngate = ML_HEADS * 4
    gw = gw.reshape(n_even, 3, ML_HEADS, dh, ngate).transpose(0, 2, 1, 3, 4)
    gw = jnp.pad(gw, ((0, 0),) * 4 + ((0, LANES - ngate),)).astype(BF16)
    gb = jnp.stack([ml_b_ig[:, 0], ml_b_ig[:, 1], ml_b_fg[:, 0], ml_b_fg[:, 1]], axis=-1).reshape(n_even, ngate)
    gb_col = jnp.pad(gb, ((0, 0), (0, LANES - ngate))).reshape(n_even, 1, LANES)
    ml_cb = ml_conv_b.reshape(n_even, 1, ml_w)
    ml_ng = ml_norm_g.reshape(n_even, 1, ml_w)

    wa_t = _block_diag_tiles(lru_w_a, MXU_DIM)
    wx_t = _block_diag_tiles(lru_w_x, MXU_DIM)
    lru_gate = jnp.concatenate([wa_t[:, 0], wx_t[:, 0], wa_t[:, 1], wx_t[:, 1]], axis=-1).astype(BF16)
    groups = lru_w // MXU_DIM
    ba = lru_b_a.reshape(n_even, 2, groups, MXU_DIM)
    bx = lru_b_x.reshape(n_even, 2, groups, MXU_DIM)
    lru_gate_b = jnp.concatenate([ba[:, 0], bx[:, 0], ba[:, 1], bx[:, 1]], axis=-1).reshape(n_even, groups, 1, 4 * MXU_DIM)
    lru_cb = lru_conv_b.reshape(n_even, 1, lru_w)

    w_hg = od_w_in[:, :, :5 * hg_w].reshape(n_odd, d, 5, HG_HEADS // 2, 2 * hg_dk).transpose(0, 3, 2, 1, 4).astype(BF16)
    w_ret = od_w_in[:, :, 5 * hg_w:].reshape(n_odd, d, 4, RET_HEADS, ret_dk).transpose(0, 3, 2, 1, 4).astype(BF16)
    hg_ng = hg_norm_g.reshape(n_odd, 1, hg_w)
    ret_ng = ret_norm_g.reshape(n_odd, 1, ret_w)
    lbs = _lower_bounds(hg_lb_logits).reshape(depth, 1, hg_w)

    def trunk(x3):
        bsz, seq, _ = x3.shape
        n = bsz * seq
        cos, sin = _rope_tables(seq, ret_dk)
        x = x3.reshape(n, d)
        for l in range(depth):
            x, xb = _ffn_ln(x, fa_in, fa_out, l, lng, lnb, 0, alpha, True)
            xb3 = xb.reshape(bsz, seq, d)
            if l % 2 == 0:
                e = l // 2
                q, k, v, gc = _ml_prep(xb3, ev_in, e, ml_conv_w, ml_cb, wq_t, wk_t, wv_t, gw, gb_col)
                cf, cb = _ml_gates(gc)
                y1 = _ml_core(q, k, v, gc, cf, cb, xb3, ev_in, e, ml_ng)
                y2 = _deinterleave(_lru(_interleave(xb3, LRU_SEGS), ev_in, e, lru_conv_w, lru_cb, lru_gate,
                                        lru_gate_b, lru_lambda), LRU_SEGS)
                w_out, idx = ev_out, e
            else:
                o = l // 2
                y1 = _hgrn2(xb3, w_hg, o, lbs, l, hg_ng)
                y2 = _retention(xb3, w_ret, o, cos, sin, ret_ng)
                w_out, idx = od_out, o
            x = _proj_ln(x, y1.reshape(n, -1), y2.reshape(n, -1), w_out, idx, lng, lnb, l * 3 + 1, alpha)
            (x,) = _ffn_ln(x, fb_in, fb_out, l, lng, lnb, 2, alpha, False)
        return x.reshape(bsz, seq, d)

    return (trunk(x_prompt), trunk(x_sample))
```

```python
import functools
import math

import jax
import jax.numpy as jnp
from jax import lax
from jax.experimental import pallas as pl
from jax.experimental.pallas import tpu as pltpu

F32 = jnp.float32
BF16 = jnp.bfloat16

LN_EPS = 1e-5
LOG2E = 1.4426950408889634
CONV_W = 4
CONV_PAD_LEFT = 2
ML_HEADS = 4
ML_QKV_BLOCK = 4
LRU_BLOCKS = 16
LRU_C = 8.0
HG_HEADS = 8
RET_HEADS = 4
ROPE_BASE = 10000.0

LANES = 128
MXU_DIM = 256

TOKEN_TILE = 512
FFN_TOKEN_TILE = 1024
FFN_CHUNK = 1408
FFN_ROW_GROUPS = 4
ML_CHUNK = MXU_DIM
RET_CHUNK = MXU_DIM
LRU_SEGS = 8
LRU_UNROLL = 8
LRU_TINY = 1e-37
LRU_SERIES_BELOW = 2.0 ** -9
ROW_BLOCK = MXU_DIM
CHUNK_UNROLL = 2
WIDE_UNROLL = 4
HG_CHUNK = LANES
HG_SAFE_DECAY = 64.0
VMEM_LIMIT = 56 * 1024 * 1024


def _mm(a, b):
    return jnp.dot(a, b, preferred_element_type=F32)


def _mm_nt(a, b):
    return lax.dot_general(a, b, (((1,), (1,)), ((), ())), preferred_element_type=F32)


def _mm_tn(a, b):
    return lax.dot_general(a, b, (((0,), (0,)), ((), ())), preferred_element_type=F32)


def _split3(x):
    x1 = x.astype(BF16)
    r = x - x1.astype(F32)
    x2 = r.astype(BF16)
    r = r - x2.astype(F32)
    return x1, x2, r.astype(BF16)


def _tri_left(tri, x):
    x1, x2, x3 = _split3(x)
    return _mm(tri, x1) + _mm(tri, x2) + _mm(tri, x3)


def _tri_right(x, tri):
    x1, x2, x3 = _split3(x)
    return _mm(x1, tri) + _mm(x2, tri) + _mm(x3, tri)


def _tri(n, lower):
    r = lax.broadcasted_iota(jnp.int32, (n, n), 0)
    c = lax.broadcasted_iota(jnp.int32, (n, n), 1)
    return r >= c if lower else r <= c


def _sigmoid(x):
    return 1.0 / (1.0 + jnp.exp(-x))


def _sigmoid_t(x):
    return 0.5 * jnp.tanh(0.5 * x) + 0.5


def _silu(x):
    return x * _sigmoid(x)


def _log_sigmoid(x):
    return jnp.minimum(x, 0.0) - jnp.log(1.0 + jnp.exp(-jnp.abs(x)))


def _layer_norm(y, g, b):
    mu = jnp.mean(y, axis=-1, keepdims=True)
    d = y - mu
    var = jnp.mean(d * d, axis=-1, keepdims=True)
    return d * lax.rsqrt(var + LN_EPS) * g + b


def _dwconv(u, w, b):
    n = u.shape[0]
    row = lax.broadcasted_iota(jnp.int32, u.shape, 0)
    acc = u * w[CONV_PAD_LEFT:CONV_PAD_LEFT + 1] + b
    for j in range(CONV_W):
        off = j - CONV_PAD_LEFT
        if off == 0:
            continue
        shifted = pltpu.roll(u, (-off) % n, 0)
        valid = (row + off >= 0) & (row + off < n)
        acc = acc + jnp.where(valid, shifted, 0.0) * w[j:j + 1]
    return acc


def _col(x, idx):
    lane = lax.broadcasted_iota(jnp.int32, x.shape, 1)
    return jnp.sum(jnp.where(lane == idx, x, 0.0), axis=1, keepdims=True)


def _row(x, idx):
    sub = lax.broadcasted_iota(jnp.int32, x.shape, 0)
    return jnp.sum(jnp.where(sub == idx, x, 0.0), axis=0, keepdims=True)


def _params(*sem):
    return pltpu.CompilerParams(dimension_semantics=sem, vmem_limit_bytes=VMEM_LIMIT)


def _lbs_kernel(logit_ref, o_ref):
    z = logit_ref[...]
    e = jnp.exp(z - jnp.max(z, axis=0, keepdims=True))
    p = e / jnp.sum(e, axis=0, keepdims=True)
    depth = z.shape[0]
    run = jnp.zeros_like(p[0:1])
    for l in range(1, depth):
        run = run + p[l:l + 1]
        o_ref[l:l + 1, :] = run
    o_ref[0:1, :] = jnp.zeros_like(run)


def _lower_bounds(logits):
    return pl.pallas_call(_lbs_kernel, out_shape=jax.ShapeDtypeStruct(logits.shape, F32))(logits.astype(F32))


def _ffn_kernel(x_ref, win_ref, wout_ref, g_ref, b_ref, *o_refs, d_ff, alpha):
    rows = x_ref.shape[0] // FFN_ROW_GROUPS
    for r0 in range(0, x_ref.shape[0], rows):
        x = x_ref[r0:r0 + rows, :]
        xb = x.astype(BF16)
        acc = jnp.zeros(x.shape, F32)
        for c0 in range(0, d_ff, FFN_CHUNK):
            gate = _mm(xb, win_ref[:, c0:c0 + FFN_CHUNK])
            up = _mm(xb, win_ref[:, d_ff + c0:d_ff + c0 + FFN_CHUNK])
            h = (_silu(gate) * up).astype(BF16)
            acc = acc + _mm(h, wout_ref[c0:c0 + FFN_CHUNK, :])
        y = _layer_norm(alpha * x + 0.5 * acc, g_ref[...], b_ref[...])
        o_refs[0][r0:r0 + rows, :] = y
        if len(o_refs) > 1:
            o_refs[1][r0:r0 + rows, :] = y.astype(BF16)


def _ffn_ln(x, w_in, w_out, layer, ln_g, ln_b, ln_idx, alpha, with_bf16):
    n, d = x.shape
    d_ff = w_out.shape[1]
    tm = min(FFN_TOKEN_TILE, n)
    assert n % tm == 0 and d_ff % FFN_CHUNK == 0
    out_shape = [jax.ShapeDtypeStruct((n, d), F32)]
    out_specs = [pl.BlockSpec((tm, d), lambda i: (i, 0))]
    if with_bf16:
        out_shape.append(jax.ShapeDtypeStruct((n, d), BF16))
        out_specs.append(pl.BlockSpec((tm, d), lambda i: (i, 0)))
    const = dict(pipeline_mode=pl.Buffered(1))
    return pl.pallas_call(
        functools.partial(_ffn_kernel, d_ff=d_ff, alpha=alpha),
        grid=(n // tm,),
        in_specs=[
            pl.BlockSpec((tm, d), lambda i: (i, 0)),
            pl.BlockSpec((None, d, 2 * d_ff), lambda i: (layer, 0, 0), **const),
            pl.BlockSpec((None, d_ff, d), lambda i: (layer, 0, 0), **const),
            pl.BlockSpec((None, 1, d), lambda i: (layer * 3 + ln_idx, 0, 0)),
            pl.BlockSpec((None, 1, d), lambda i: (layer * 3 + ln_idx, 0, 0)),
        ],
        out_specs=out_specs,
        out_shape=out_shape,
        compiler_params=_params("parallel"),
        name="ffn_ln",
    )(x, w_in, w_out, ln_g, ln_b)


def _proj_ln_kernel(x_ref, y1_ref, y2_ref, w_ref, g_ref, b_ref, o_ref, *, alpha):
    w1 = y1_ref.shape[1]
    rows = x_ref.shape[0] // FFN_ROW_GROUPS
    for r0 in range(0, x_ref.shape[0], rows):
        rs = slice(r0, r0 + rows)
        acc = _mm(y1_ref[rs, :], w_ref[0:w1, :]) + _mm(y2_ref[rs, :], w_ref[w1:, :])
        o_ref[rs, :] = _layer_norm(alpha * x_ref[rs, :] + acc, g_ref[...], b_ref[...])


def _proj_ln(x, y1, y2, w_out, idx, ln_g, ln_b, ln_row, alpha):
    n, d = x.shape
    w1, w2 = y1.shape[1], y2.shape[1]
    tm = min(TOKEN_TILE, n)
    return pl.pallas_call(
        functools.partial(_proj_ln_kernel, alpha=alpha),
        grid=(n // tm,),
        in_specs=[
            pl.BlockSpec((tm, d), lambda i: (i, 0)),
            pl.BlockSpec((tm, w1), lambda i: (i, 0)),
            pl.BlockSpec((tm, w2), lambda i: (i, 0)),
            pl.BlockSpec((None, w1 + w2, d), lambda i: (idx, 0, 0)),
            pl.BlockSpec((None, 1, d), lambda i: (ln_row, 0, 0)),
            pl.BlockSpec((None, 1, d), lambda i: (ln_row, 0, 0)),
        ],
        out_specs=pl.BlockSpec((tm, d), lambda i: (i, 0)),
        out_shape=jax.ShapeDtypeStruct((n, d), F32),
        compiler_params=_params("parallel"),
        name="proj_ln",
    )(x, y1, y2, w_out, ln_g, ln_b)


def _ml_prep_kernel(xb_ref, wu_ref, cw_ref, cb_ref, wq_ref, wk_ref, wv_ref, gw_ref, bcol_ref,
                    q_ref, k_ref, v_ref, gc_ref, c_ref, u_ref):
    u = _mm(xb_ref[...], wu_ref[...])
    c_ref[...] = _silu(_dwconv(u, cw_ref[...], cb_ref[...])).astype(BF16)
    u_ref[...] = u.astype(BF16)

    @pl.when(pl.program_id(1) == 0)
    def _():
        gc_ref[...] = jnp.broadcast_to(bcol_ref[...], gc_ref.shape)

    def block(i, carry):
        rows = pl.ds(pl.multiple_of(i * ROW_BLOCK, ROW_BLOCK), ROW_BLOCK)
        c = c_ref[rows, :]
        qb = _mm(c, wq_ref[...]).astype(BF16)
        kb = _mm(c, wk_ref[...]).astype(BF16)
        vb = _mm(u_ref[rows, :], wv_ref[...]).astype(BF16)
        q_ref[rows, :] = qb
        k_ref[rows, :] = kb
        v_ref[rows, :] = vb
        gc_ref[rows, :] += _mm(qb, gw_ref[0]) + _mm(kb, gw_ref[1]) + _mm(vb, gw_ref[2])
        return carry

    lax.fori_loop(0, q_ref.shape[0] // ROW_BLOCK, block, 0, unroll=CHUNK_UNROLL)


def _ml_prep(xb, w_in, e, cw, cb, wq, wk, wv, gw, bcol):
    bsz, seq, d = xb.shape
    dh = wq.shape[-1]
    heads = ML_HEADS
    qkv = jax.ShapeDtypeStruct((bsz, seq, heads * dh), BF16)
    head_blk = pl.BlockSpec((None, seq, dh), lambda b, h: (b, 0, h))
    return pl.pallas_call(
        _ml_prep_kernel,
        grid=(bsz, heads),
        in_specs=[
            pl.BlockSpec((None, seq, d), lambda b, h: (b, 0, 0)),
            pl.BlockSpec((None, d, dh), lambda b, h: (e, 0, h)),
            pl.BlockSpec((None, CONV_W, dh), lambda b, h: (e, 0, h)),
            pl.BlockSpec((None, 1, dh), lambda b, h: (e, 0, h)),
            pl.BlockSpec((None, None, dh, dh), lambda b, h: (e, h, 0, 0)),
            pl.BlockSpec((None, None, dh, dh), lambda b, h: (e, h, 0, 0)),
            pl.BlockSpec((None, None, dh, dh), lambda b, h: (e, h, 0, 0)),
            pl.BlockSpec((None, None, 3, dh, LANES), lambda b, h: (e, h, 0, 0, 0)),
            pl.BlockSpec((None, 1, LANES), lambda b, h: (e, 0, 0)),
        ],
        out_specs=[head_blk, head_blk, head_blk,
                   pl.BlockSpec((None, seq, LANES), lambda b, h: (b, 0, 0))],
        out_shape=[qkv, qkv, qkv, jax.ShapeDtypeStruct((bsz, seq, LANES), F32)],
        scratch_shapes=[pltpu.VMEM((seq, dh), BF16), pltpu.VMEM((seq, dh), BF16)],
        compiler_params=_params("parallel", "arbitrary"),
        name="mlstm_prep",
    )(xb, w_in, cw, cb, wq, wk, wv, gw, bcol)


def _ml_gates_kernel(gc_ref, cf_ref, cb_ref):
    ck = ML_CHUNK
    tri_l_b = jnp.where(_tri(ck, True), 1.0, 0.0).astype(BF16)
    tri_u_b = jnp.where(_tri(ck, False), 1.0, 0.0).astype(BF16)

    def step(c, carry):
        rows = pl.ds(pl.multiple_of(c * ck, ck), ck)
        lf = _log_sigmoid(gc_ref[rows, :]) * LOG2E
        cf_ref[rows, :] = _tri_left(tri_l_b, lf)
        cb_ref[rows, :] = _tri_left(tri_u_b, lf)
        return carry

    lax.fori_loop(0, gc_ref.shape[0] // ck, step, 0)


def _ml_gates(gc):
    bsz, seq, w = gc.shape
    blk = pl.BlockSpec((None, seq, w), lambda b: (b, 0, 0))
    out = jax.ShapeDtypeStruct(gc.shape, F32)
    return pl.pallas_call(
        _ml_gates_kernel, grid=(bsz,), in_specs=[blk], out_specs=[blk, blk], out_shape=[out, out],
        compiler_params=_params("parallel"), name="mlstm_gates",
    )(gc)


def _ml_core_kernel(q_ref, k_ref, v_ref, gc_ref, cf_ref, cb_ref, gr_ref, cfr_ref, cbr_ref, xb_ref, wz_ref, ng_ref,
                    y_ref, h_ref, col_ref, c_ref, n_ref, neg_ref):
    head = pl.program_id(1)
    seq, dh = q_ref.shape
    ck = ML_CHUNK
    nchunks = seq // ck
    wide = lambda x: jnp.concatenate([x] * (dh // LANES), axis=1)
    cum_src = (cf_ref, cb_ref)
    cum_row_src = (cfr_ref, cbr_ref)
    for dirn in range(2):
        col_ref[dirn] = jnp.broadcast_to(_col(cum_src[dirn][...], head * 4 + 2 + dirn), (seq, LANES))
        col_ref[2 + dirn] = jnp.broadcast_to(_col(gc_ref[...], head * 4 + dirn) * LOG2E, (seq, LANES))
        neg_ref[dirn] = jnp.where(_tri(ck, dirn == 0), 0.0, -jnp.inf)
    c_ref[...] = jnp.zeros(c_ref.shape, F32)
    n_ref[...] = jnp.zeros(n_ref.shape, F32)
    h_ref[...] = jnp.zeros(h_ref.shape, F32)

    def step(it, m_state):
        m_out = []
        for dirn in range(2):
            backward = dirn == 1
            m_prev = m_state[dirn]
            j = (nchunks - 1 - it) if backward else it
            rows = pl.ds(pl.multiple_of(j * ck, ck), ck)
            qc = q_ref[rows, :]
            kc = k_ref[rows, :]
            vc = v_ref[rows, :]
            cum_col = col_ref[dirn, rows, :]
            i_col = col_ref[2 + dirn, rows, :]
            cum_row = _row(cum_row_src[dirn][j], head * 4 + 2 + dirn)
            i_row = _row(gr_ref[j], head * 4 + dirn) * LOG2E
            far = 0 if backward else ck - 1
            total = cum_row[:, far:far + 1]

            d_log = wide(cum_col) + (i_row - cum_row) + neg_ref[dirn]
            inter_log = cum_col + m_prev
            m_t = jnp.maximum(inter_log, jnp.max(d_log, axis=1, keepdims=True))
            p = _mm_nt(qc, kc) * jnp.exp2(d_log - wide(m_t))
            w_inter = jnp.exp2(inter_log - m_t)
            num = _mm(p.astype(BF16), vc) + wide(w_inter) * _mm(qc, c_ref[dirn].astype(BF16))
            qn = jnp.sum(qc.astype(F32) * n_ref[dirn], axis=1, keepdims=True)
            den = jnp.sum(p, axis=1, keepdims=True) + w_inter * qn
            h_ref[rows, :] += num / wide(jnp.maximum(jnp.abs(den), jnp.exp2(-m_t)))

            g_col = total - cum_col + i_col
            g_row = total - cum_row + i_row
            m_new = jnp.maximum(total + m_prev, jnp.max(g_row, axis=1, keepdims=True))
            kw = kc.astype(F32) * wide(jnp.exp2(g_col - m_new))
            decay = jnp.exp2(total + m_prev - m_new)
            c_ref[dirn] = decay * c_ref[dirn] + _mm_tn(kw.astype(BF16), vc)
            n_ref[dirn] = decay * n_ref[dirn] + jnp.sum(kw, axis=0, keepdims=True)
            m_out.append(m_new)
        return tuple(m_out)

    lax.fori_loop(0, nchunks, step, (jnp.zeros((1, 1), F32), jnp.zeros((1, 1), F32)), unroll=CHUNK_UNROLL)
    h = h_ref[...]
    h = h - jnp.mean(h, axis=-1, keepdims=True)
    h = h * lax.rsqrt(jnp.mean(h * h, axis=-1, keepdims=True) + LN_EPS) * ng_ref[...]
    z = _mm(xb_ref[...], wz_ref[...])
    y_ref[...] = (_sigmoid_t(z) * h).astype(BF16)


def _ml_core(q, k, v, gc, cf, cb, xb, w_in, e, norm_g):
    bsz, seq, width = q.shape
    d = xb.shape[-1]
    heads = ML_HEADS
    dh = width // heads
    nchunks = seq // ML_CHUNK
    ng = heads * 4

    def rows(a):
        return a[:, :, :ng].reshape(bsz, nchunks, ML_CHUNK, ng).transpose(0, 1, 3, 2)

    head_blk = pl.BlockSpec((None, seq, dh), lambda b, h: (b, 0, h))
    col_blk = pl.BlockSpec((None, seq, LANES), lambda b, h: (b, 0, 0))
    row_blk = pl.BlockSpec((None, nchunks, ng, ML_CHUNK), lambda b, h: (b, 0, 0, 0))
    return pl.pallas_call(
        _ml_core_kernel,
        grid=(bsz, heads),
        in_specs=[
            head_blk, head_blk, head_blk,
            col_blk, col_blk, col_blk,
            row_blk, row_blk, row_blk,
            pl.BlockSpec((None, seq, d), lambda b, h: (b, 0, 0)),
            pl.BlockSpec((None, d, dh), lambda b, h: (e, 0, heads + h)),
            pl.BlockSpec((None, 1, dh), lambda b, h: (e, 0, h)),
        ],
        out_specs=head_blk,
        out_shape=jax.ShapeDtypeStruct((bsz, seq, width), BF16),
        scratch_shapes=[pltpu.VMEM((seq, dh), F32), pltpu.VMEM((4, seq, LANES), F32),
                        pltpu.VMEM((2, dh, dh), F32), pltpu.VMEM((2, 1, dh), F32),
                        pltpu.VMEM((2, ML_CHUNK, ML_CHUNK), F32)],
        compiler_params=_params("parallel", "arbitrary"),
        name="mlstm_core",
    )(q, k, v, gc, cf, cb, rows(gc), rows(cf), rows(cb), xb, w_in, norm_g)


def _dwconv_interleaved(u, w, b, nseg):
    n = u.shape[0]
    acc = u * w[CONV_PAD_LEFT:CONV_PAD_LEFT + 1] + b
    for j in range(CONV_W):
        off = j - CONV_PAD_LEFT
        if off == 0:
            continue
        k = nseg * abs(off)
        seg = lax.broadcasted_iota(jnp.int32, (k, u.shape[1]), 0) & (nseg - 1)
        if off < 0:
            edge = jnp.where(seg != 0, pltpu.roll(u[n - k:], 1, 0), 0.0)
            shifted = jnp.concatenate([edge, u[:n - k]], axis=0)
        else:
            edge = jnp.where(seg != nseg - 1, pltpu.roll(u[:k], k - 1, 0), 0.0)
            shifted = jnp.concatenate([u[k:], edge], axis=0)
        acc = acc + shifted * w[j:j + 1]
    return acc


def _lru_kernel(xb_ref, wu_ref, wg_ref, cw_ref, cb_ref, wgate_ref, bgate_ref, lam_ref, y_ref, a_ref, b_ref, x_ref):
    _, n, w = a_ref.shape
    nseg = LRU_SEGS
    steps = n // nseg
    x_ref[...] = _dwconv_interleaved(_mm(xb_ref[...], wu_ref[...]), cw_ref[...], cb_ref[...], nseg)
    lam = lam_ref[...]
    sp = jnp.maximum(-lam, 0.0) + jnp.log1p(jnp.exp(-jnp.abs(lam)))

    def gates(blk, carry):
        rows = pl.ds(pl.multiple_of(blk * ROW_BLOCK, ROW_BLOCK), ROW_BLOCK)
        x = x_ref[rows, :]
        pre = _mm(x.astype(BF16), wgate_ref[...]) + bgate_ref[...]
        for dirn in range(2):
            r = _sigmoid_t(pre[:, (2 * dirn) * w:(2 * dirn + 1) * w])
            i = _sigmoid_t(pre[:, (2 * dirn + 1) * w:(2 * dirn + 2) * w])
            log_a = -LRU_C * r * sp[dirn:dirn + 1]
            a = jnp.exp(log_a)
            a_ref[dirn, rows, :] = a
            series = log_a * ((log_a * (-4.0 / 3.0) - 2.0) * log_a - 2.0)
            v = jnp.where(log_a > -LRU_SERIES_BELOW, series, 1.0 - a * a)
            b_ref[dirn, rows, :] = v * lax.rsqrt(jnp.maximum(v, LRU_TINY)) * i * x
        return carry

    lax.fori_loop(0, n // ROW_BLOCK, gates, 0, unroll=CHUNK_UNROLL)

    def step(it, carry):
        out = []
        for dirn in range(2):
            h, p = carry[2 * dirn], carry[2 * dirn + 1]
            j = (steps - 1 - it) if dirn else it
            rows = pl.ds(pl.multiple_of(j * nseg, nseg), nseg)
            a = a_ref[dirn, rows, :]
            h = a * h + b_ref[dirn, rows, :]
            p = p * a
            b_ref[dirn, rows, :] = h
            a_ref[dirn, rows, :] = p
            out += [h, p]
        return tuple(out)

    zeros = jnp.zeros((nseg, w), F32)
    ones = jnp.ones((nseg, w), F32)
    h_f, p_f, h_b, p_b = lax.fori_loop(0, steps, step, (zeros, ones, zeros, ones), unroll=LRU_UNROLL)
    srow = lax.broadcasted_iota(jnp.int32, (nseg, w), 0)
    c_f, c_b = zeros, zeros
    for _ in range(nseg - 1):
        c_f = jnp.where(srow >= 1, pltpu.roll(h_f + p_f * c_f, 1, 0), 0.0)
        c_b = jnp.where(srow <= nseg - 2, pltpu.roll(h_b + p_b * c_b, nseg - 1, 0), 0.0)
    def finish(blk, carry):
        rows = pl.ds(pl.multiple_of(blk * ROW_BLOCK, ROW_BLOCK), ROW_BLOCK)
        split = lambda ref, d: ref[d, rows, :].reshape(ROW_BLOCK // nseg, nseg, w)
        h = split(b_ref, 0) + split(a_ref, 0) * c_f[None] + split(b_ref, 1) + split(a_ref, 1) * c_b[None]
        g = _mm(xb_ref[rows, :], wg_ref[...])
        y_ref[rows, :] = (h.reshape(ROW_BLOCK, w) * jax.nn.gelu(g, approximate=True)).astype(BF16)
        return carry

    lax.fori_loop(0, n // ROW_BLOCK, finish, 0, unroll=CHUNK_UNROLL)


def _lru(xb, w_in, e, cw, cb, wgate, bgate, lam):
    bsz, seq, d = xb.shape
    groups, w = wgate.shape[1], wgate.shape[2]
    ublk = 2 * ML_HEADS
    return pl.pallas_call(
        _lru_kernel,
        grid=(bsz, groups),
        in_specs=[
            pl.BlockSpec((None, seq, d), lambda b, j: (b, 0, 0)),
            pl.BlockSpec((None, d, w), lambda b, j: (e, 0, ublk + j)),
            pl.BlockSpec((None, d, w), lambda b, j: (e, 0, ublk + groups + j)),
            pl.BlockSpec((None, CONV_W, w), lambda b, j: (e, 0, j)),
            pl.BlockSpec((None, 1, w), lambda b, j: (e, 0, j)),
            pl.BlockSpec((None, None, w, 4 * w), lambda b, j: (e, j, 0, 0)),
            pl.BlockSpec((None, None, 1, 4 * w), lambda b, j: (e, j, 0, 0)),
            pl.BlockSpec((None, 2, w), lambda b, j: (e, 0, j)),
        ],
        out_specs=pl.BlockSpec((None, seq, w), lambda b, j: (b, 0, j)),
        out_shape=jax.ShapeDtypeStruct((bsz, seq, groups * w), BF16),
        scratch_shapes=[pltpu.VMEM((2, seq, w), F32), pltpu.VMEM((2, seq, w), F32), pltpu.VMEM((seq, w), F32)],
        compiler_params=_params("parallel", "arbitrary"),
        name="rglru",
    )(xb, w_in, w_in, cw, cb, wgate, bgate, lam)


def _interleave(x, nseg):
    bsz, seq, w = x.shape
    return x.reshape(bsz, nseg, seq // nseg, w).transpose(0, 2, 1, 3).reshape(bsz, seq, w)


def _deinterleave(x, nseg):
    bsz, seq, w = x.shape
    return x.reshape(bsz, seq // nseg, nseg, w).transpose(0, 2, 1, 3).reshape(bsz, seq, w)


def _seg_bounds(cum, h, backward):
    n, w = cum.shape
    zero = jnp.zeros((1, w), F32)
    his, los = [], []
    for s0 in range(0, n, h):
        e0 = s0 + h - 1
        if backward:
            hi = cum[e0 + 1:e0 + 2] if e0 + 1 < n else zero
            lo = cum[s0:s0 + 1]
        else:
            hi = cum[s0 - 1:s0] if s0 > 0 else zero
            lo = cum[e0:e0 + 1]
        his.append(jnp.broadcast_to(hi, (h, w)))
        los.append(jnp.broadcast_to(lo, (h, w)))
    if len(his) == 1:
        return his[0], los[0]
    return jnp.concatenate(his, axis=0), jnp.concatenate(los, axis=0)


def _pair_rhs(x, dk):
    lane = lax.broadcasted_iota(jnp.int32, x.shape, 1)
    z = jnp.zeros_like(x)
    return jnp.concatenate([jnp.where(lane < dk, x, z), jnp.where(lane >= dk, x, z)], axis=0)


def _hg_variant(d32, d64, d128):
    use128 = d128 < HG_SAFE_DECAY
    use64 = jnp.logical_and(jnp.logical_not(use128), d64 < HG_SAFE_DECAY)
    use32 = jnp.logical_and(jnp.logical_not(jnp.logical_or(use128, use64)), d32 < HG_SAFE_DECAY)
    use1 = jnp.logical_not(jnp.logical_or(jnp.logical_or(use128, use64), use32))
    return {128: use128, 64: use64, 32: use32, 1: use1}


def _hg_kernel(xb_ref, w_ref, lb_ref, ng_ref, y_ref, q_ref, k_ref, g_ref, cum_ref, v_ref, o_ref, s_ref, gate_ref):
    seq, dk2 = q_ref.shape
    dk = dk2 // 2
    ck = HG_CHUNK
    nchunks = seq // ck
    lb = lb_ref[...]

    def project(i, carry):
        rows = pl.ds(pl.multiple_of(i * ROW_BLOCK, ROW_BLOCK), ROW_BLOCK)
        xb = xb_ref[rows, :]
        q_ref[rows, :] = _silu(_mm(xb, w_ref[0]))
        v_ref[rows, :] = _mm(xb, w_ref[3]).astype(BF16)
        for dirn in range(2):
            f = lb + (1.0 - lb) * _sigmoid(_mm(xb, w_ref[1 + dirn]))
            k_ref[dirn, rows, :] = 1.0 - f
            g_ref[dirn, rows, :] = jnp.log(f)
        gate_ref[rows, :] = _silu(_mm(xb, w_ref[4]))
        return carry

    lax.fori_loop(0, seq // ROW_BLOCK, project, 0, unroll=CHUNK_UNROLL)
    tri = (_tri(ck, True), _tri(ck, False))
    tri_b = tuple(jnp.where(m, 1.0, 0.0).astype(BF16) for m in tri)
    rr = lax.broadcasted_iota(jnp.int32, (ck, ck), 0)
    cc = lax.broadcasted_iota(jnp.int32, (ck, ck), 1)
    blk_xor = rr ^ cc
    trow = lax.broadcasted_iota(jnp.int32, (ck, dk2), 0)
    tlane = lax.broadcasted_iota(jnp.int32, (ck, dk2), 1)
    srow = lax.broadcasted_iota(jnp.int32, (dk2, dk2), 0)
    scol = lax.broadcasted_iota(jnp.int32, (dk2, dk2), 1)
    state_diag = (srow < dk) == (scol < dk)

    def cum_step(c, worst):
        r0 = pl.multiple_of(c * ck, ck)
        w32, w64, w128 = worst
        for dirn in range(2):
            cum = _tri_left(tri_b[dirn], g_ref[dirn, pl.ds(r0, ck), :])
            cum_ref[dirn, pl.ds(r0, ck), :] = cum
            ends = [ck - (i + 1) * 32 if dirn else (i + 1) * 32 - 1 for i in range(ck // 32)]
            c0, c1, c2, c3 = [cum[e:e + 1, :] for e in ends]
            w32 = jnp.maximum(w32, jnp.maximum(jnp.maximum(-c0, c0 - c1), jnp.maximum(c1 - c2, c2 - c3)))
            w64 = jnp.maximum(w64, jnp.maximum(-c1, c1 - c3))
            w128 = jnp.maximum(w128, -c3)
        return w32, w64, w128

    zero_row = jnp.zeros((1, dk2), F32)
    w32, w64, w128 = lax.fori_loop(0, nchunks, cum_step, (zero_row, zero_row, zero_row), unroll=WIDE_UNROLL)
    variant = _hg_variant(jnp.max(w32), jnp.max(w64), jnp.max(w128))

    def run(blk):
        s_ref[...] = jnp.zeros(s_ref.shape, F32)
        o_ref[...] = jnp.zeros(o_ref.shape, F32)
        big_levels = [h for h in (64, 32, 16, 8) if blk <= h < ck]
        small_levels = [h for h in (1, 2, 4) if h >= blk]
        pair = lambda m: jnp.concatenate([m, m], axis=1)
        in_blk = [pair(jnp.logical_and(tri[d], blk_xor < blk)) for d in range(2)] if blk > 1 else None
        same2h = {h: pair(blk_xor < 2 * h) for h in big_levels + small_levels}

        def add(att, part):
            return part if att is None else att + part

        def step(it, carry):
            for dirn in range(2):
                backward = dirn == 1
                j = (nchunks - 1 - it) if backward else it
                r0 = pl.multiple_of(j * ck, ck)
                qc = q_ref[pl.ds(r0, ck), :]
                kc = k_ref[dirn, pl.ds(r0, ck), :]
                cum = cum_ref[dirn, pl.ds(r0, ck), :]
                vc = v_ref[pl.ds(r0, ck), :]
                q_in = (qc * jnp.exp(cum)).astype(BF16)
                att = None
                if blk == ck:
                    kt = (kc * jnp.exp(-cum)).astype(BF16)
                    att = jnp.where(in_blk[dirn], _mm_nt(q_in, _pair_rhs(kt, dk)), 0.0)
                elif blk > 1:
                    ref, _ = _seg_bounds(cum, blk, backward)
                    qt = (qc * jnp.exp(cum - ref)).astype(BF16)
                    kt = (kc * jnp.exp(ref - cum)).astype(BF16)
                    att = jnp.where(in_blk[dirn], _mm_nt(qt, _pair_rhs(kt, dk)), 0.0)
                for h in big_levels:
                    seg_hi, seg_lo = _seg_bounds(cum, h, backward)
                    q_side = ((trow & h) != 0) != backward
                    qt = jnp.where(q_side, qc * jnp.exp(cum - seg_hi), 0.0).astype(BF16)
                    kt = jnp.where(q_side, 0.0, kc * jnp.exp(seg_lo - cum)).astype(BF16)
                    att = add(att, jnp.where(same2h[h], _mm_nt(qt, _pair_rhs(kt, dk)), 0.0))
                if small_levels:
                    seg_hi = cum - g_ref[dirn, pl.ds(r0, ck), :]
                    seg_lo = cum
                    for h in small_levels:
                        if h > 1:
                            bit = (trow & (h // 2)) != 0
                            seg_hi = jnp.maximum(seg_hi, jnp.where(bit, pltpu.roll(seg_hi, h // 2, 0),
                                                                   pltpu.roll(seg_hi, ck - h // 2, 0)))
                            seg_lo = jnp.minimum(seg_lo, jnp.where(bit, pltpu.roll(seg_lo, h // 2, 0),
                                                                   pltpu.roll(seg_lo, ck - h // 2, 0)))
                        q_side = ((trow & h) != 0) != backward
                        qt = jnp.where(q_side, qc * jnp.exp(cum - seg_hi), 0.0).astype(BF16)
                        kt = jnp.where(q_side, 0.0, kc * jnp.exp(seg_lo - cum)).astype(BF16)
                        att = add(att, jnp.where(same2h[h], _mm_nt(qt, _pair_rhs(kt, dk)), 0.0))
                oc = _mm(att.astype(BF16), _pair_rhs(vc, dk)) + _mm_nt(q_in, s_ref[dirn].astype(BF16))
                if blk == 1:
                    prod = qc * kc
                    da = jnp.sum(prod[:, :dk], axis=1, keepdims=True)
                    db = jnp.sum(prod[:, dk:], axis=1, keepdims=True)
                    oc = oc + jnp.where(tlane < dk, da, db) * vc.astype(F32)
                o_ref[pl.ds(r0, ck), :] += oc
                far = 0 if backward else ck - 1
                total = cum[far:far + 1, :]
                kd = (kc * jnp.exp(total - cum)).astype(BF16)
                s_ref[dirn] = jnp.where(state_diag, jnp.exp(total) * s_ref[dirn] + _mm_tn(vc, kd), 0.0)
            return carry

        lax.fori_loop(0, nchunks, step, 0, unroll={128: WIDE_UNROLL, 1: 1}.get(blk, CHUNK_UNROLL))

    for blk, use in variant.items():
        pl.when(use)(functools.partial(run, blk))

    o = o_ref[...]
    ng = ng_ref[...]
    halves = []
    for lo in (0, dk):
        oh = o[:, lo:lo + dk]
        halves.append(oh * lax.rsqrt(jnp.mean(oh * oh, axis=-1, keepdims=True) + LN_EPS) * ng[:, lo:lo + dk])
    y_ref[...] = (jnp.concatenate(halves, axis=1) * gate_ref[...]).astype(BF16)


def _hgrn2(xb, w_hg, o_idx, lbs, layer, norm_g):
    bsz, seq, d = xb.shape
    pairs, dk2 = w_hg.shape[1], w_hg.shape[-1]
    wide = lambda: pltpu.VMEM((seq, dk2), F32)
    both = lambda: pltpu.VMEM((2, seq, dk2), F32)
    return pl.pallas_call(
        _hg_kernel,
        grid=(bsz, pairs),
        in_specs=[
            pl.BlockSpec((None, seq, d), lambda b, h: (b, 0, 0)),
            pl.BlockSpec((None, None, 5, d, dk2), lambda b, h: (o_idx, h, 0, 0, 0)),
            pl.BlockSpec((None, 1, dk2), lambda b, h: (layer, 0, h)),
            pl.BlockSpec((None, 1, dk2), lambda b, h: (o_idx, 0, h)),
        ],
        out_specs=pl.BlockSpec((None, seq, dk2), lambda b, h: (b, 0, h)),
        out_shape=jax.ShapeDtypeStruct((bsz, seq, pairs * dk2), BF16),
        scratch_shapes=[wide(), both(), both(), both(), pltpu.VMEM((seq, dk2), BF16), wide(),
                        pltpu.VMEM((2, dk2, dk2), F32), wide()],
        compiler_params=_params("parallel", "arbitrary"),
        name="hgrn2",
    )(xb, w_hg, lbs, norm_g)


def _ret_kernel(xb_ref, w_ref, cos_ref, sin_ref, ng_ref, y_ref, q_ref, k_ref, v_ref, o_ref, s_ref, *, heads):
    head = pl.program_id(1)
    seq, dk = q_ref.shape
    ck = RET_CHUNK
    nchunks = seq // ck
    half = dk // 2
    xb = xb_ref[...]
    cos = cos_ref[...]
    sin = sin_ref[...]

    def rope(t):
        t1, t2 = t[:, :half], t[:, half:]
        return jnp.concatenate([t1 * cos - t2 * sin, t1 * sin + t2 * cos], axis=-1)

    q_ref[...] = rope(_mm(xb, w_ref[0])).astype(BF16)
    k_ref[...] = rope(_mm(xb, w_ref[1])) * (1.0 / math.sqrt(dk))
    v_ref[...] = _mm(xb, w_ref[2]).astype(BF16)

    hf = jnp.full((1, 1), head, jnp.int32).astype(F32)
    lg_f = jnp.log(1.0 - jnp.exp(-(5.0 + hf) * math.log(2.0)))
    lg_b = jnp.log(1.0 - jnp.exp(-(4.0 + heads - hf) * math.log(2.0)))
    rr = lax.broadcasted_iota(jnp.int32, (ck, ck), 0)
    cc = lax.broadcasted_iota(jnp.int32, (ck, ck), 1)
    diff = (rr - cc).astype(F32)
    intra = (jnp.where(diff >= 0, jnp.exp(jnp.maximum(diff, 0.0) * lg_f), 0.0)
             + jnp.where(diff <= 0, jnp.exp(jnp.maximum(-diff, 0.0) * lg_b), 0.0))
    pos_c = lax.broadcasted_iota(jnp.int32, (ck, 1), 0).astype(F32)

    lgs = (lg_f, lg_b)
    q_scale = (jnp.exp((pos_c + 1.0) * lg_f), jnp.exp((ck - pos_c) * lg_b))
    k_scale = (jnp.exp((ck - 1.0 - pos_c) * lg_f), jnp.exp(pos_c * lg_b))
    s_ref[...] = jnp.zeros(s_ref.shape, F32)
    o_ref[...] = jnp.zeros(o_ref.shape, F32)

    def step(it, carry):
        for dirn in range(2):
            j = (nchunks - 1 - it) if dirn else it
            rows = pl.ds(pl.multiple_of(j * ck, ck), ck)
            qc = q_ref[rows, :]
            kc = k_ref[rows, :]
            vc = v_ref[rows, :]
            oc = q_scale[dirn] * _mm(qc, s_ref[dirn].astype(BF16))
            if dirn == 0:
                att = _mm_nt(qc, kc.astype(BF16)) * intra
                oc = oc + _mm(att.astype(BF16), vc)
            o_ref[rows, :] += oc
            s_ref[dirn] = jnp.exp(ck * lgs[dirn]) * s_ref[dirn] + _mm_tn((kc * k_scale[dirn]).astype(BF16), vc)
        return carry

    lax.fori_loop(0, nchunks, step, 0, unroll=WIDE_UNROLL)
    o = o_ref[...]
    o = o - jnp.mean(o, axis=-1, keepdims=True)
    o = o * lax.rsqrt(jnp.mean(o * o, axis=-1, keepdims=True) + LN_EPS) * ng_ref[...]
    y_ref[...] = (o * _silu(_mm(xb, w_ref[3]))).astype(BF16)


def _retention(xb, w_ret, o_idx, cos, sin, norm_g):
    bsz, seq, d = xb.shape
    heads, dk = w_ret.shape[1], w_ret.shape[-1]
    return pl.pallas_call(
        functools.partial(_ret_kernel, heads=heads),
        grid=(bsz, heads),
        in_specs=[
            pl.BlockSpec((None, seq, d), lambda b, h: (b, 0, 0)),
            pl.BlockSpec((None, None, 4, d, dk), lambda b, h: (o_idx, h, 0, 0, 0)),
            pl.BlockSpec((seq, dk // 2), lambda b, h: (0, 0)),
            pl.BlockSpec((seq, dk // 2), lambda b, h: (0, 0)),
            pl.BlockSpec((None, 1, dk), lambda b, h: (o_idx, 0, h)),
        ],
        out_specs=pl.BlockSpec((None, seq, dk), lambda b, h: (b, 0, h)),
        out_shape=jax.ShapeDtypeStruct((bsz, seq, heads * dk), BF16),
        scratch_shapes=[pltpu.VMEM((seq, dk), BF16), pltpu.VMEM((seq, dk), F32), pltpu.VMEM((seq, dk), BF16),
                        pltpu.VMEM((seq, dk), F32), pltpu.VMEM((2, dk, dk), F32)],
        compiler_params=_params("parallel", "arbitrary"),
        name="retention",
    )(xb, w_ret, cos, sin, norm_g)


def _block_diag_tiles(w, tile):
    *lead, n, c, _ = w.shape
    per = tile // c
    w = w.reshape(*lead, n // per, per, c, c)
    eye = jnp.eye(per, dtype=w.dtype)
    t = jnp.einsum('...ncd,nm->...ncmd', w, eye)
    return t.reshape(*lead, n // per, tile, tile)


def _rope_tables(seq, dk):
    pos = jnp.arange(seq, dtype=F32)
    inv = ROPE_BASE ** (-jnp.arange(dk // 2, dtype=F32) / (dk // 2))
    ang = pos[:, None] * inv[None, :]
    return jnp.cos(ang), jnp.sin(ang)


def kernel(x_prompt, x_sample, ffn_a_w_in, ffn_a_w_out, ffn_b_w_in, ffn_b_w_out, ln_g, ln_b, ev_w_in, ev_w_out, ml_conv_w, ml_conv_b, ml_wq, ml_wk, ml_wv, ml_w_ig, ml_b_ig, ml_w_fg, ml_b_fg, ml_norm_g, lru_conv_w, lru_conv_b, lru_w_a, lru_b_a, lru_w_x, lru_b_x, lru_lambda, od_w_in, od_w_out, hg_lb_logits, hg_norm_g, ret_norm_g):
    depth, _, d = ln_g.shape
    n_even, n_odd = ev_w_in.shape[0], od_w_in.shape[0]
    alpha = (2.0 * depth) ** 0.25
    ml_w = ml_conv_w.shape[-1]
    dh = ml_w // ML_HEADS
    lru_w = lru_conv_w.shape[-1]
    hg_w = hg_norm_g.shape[-1]
    ret_w = ret_norm_g.shape[-1]
    hg_dk = hg_w // HG_HEADS
    ret_dk = ret_w // RET_HEADS

    fa_in, fa_out = ffn_a_w_in.astype(BF16), ffn_a_w_out.astype(BF16)
    fb_in, fb_out = ffn_b_w_in.astype(BF16), ffn_b_w_out.astype(BF16)
    lng = ln_g.reshape(depth * 3, 1, d)
    lnb = ln_b.reshape(depth * 3, 1, d)
    ev_in, ev_out = ev_w_in.astype(BF16), ev_w_out.astype(BF16)
    od_out = od_w_out.astype(BF16)

    wq_t = _block_diag_tiles(ml_wq, dh).astype(BF16)
    wk_t = (_block_diag_tiles(ml_wk, dh) * (1.0 / math.sqrt(dh))).astype(BF16)
    wv_t = _block_diag_tiles(ml_wv, dh).astype(BF16)
    gw = jnp.stack([ml_w_ig[:, 0], ml_w_ig[:, 1], ml_w_fg[:, 0], ml_w_fg[:, 1]], axis=-1)
    ngate = ML_HEADS * 4
    gw = gw.reshape(n_even, 3, ML_HEADS, dh, ngate).transpose(0, 2, 1, 3, 4)
    gw = jnp.pad(gw, ((0, 0),) * 4 + ((0, LANES - ngate),)).astype(BF16)
    gb = jnp.stack([ml_b_ig[:, 0], ml_b_ig[:, 1], ml_b_fg[:, 0], ml_b_fg[:, 1]], axis=-1).reshape(n_even, ngate)
    gb_col = jnp.pad(gb, ((0, 0), (0, LANES - ngate))).reshape(n_even, 1, LANES)
    ml_cb = ml_conv_b.reshape(n_even, 1, ml_w)
    ml_ng = ml_norm_g.reshape(n_even, 1, ml_w)

    wa_t = _block_diag_tiles(lru_w_a, MXU_DIM)
    wx_t = _block_diag_tiles(lru_w_x, MXU_DIM)
    lru_gate = jnp.concatenate([wa_t[:, 0], wx_t[:, 0], wa_t[:, 1], wx_t[:, 1]], axis=-1).astype(BF16)
    groups = lru_w // MXU_DIM
    ba = lru_b_a.reshape(n_even, 2, groups, MXU_DIM)
    bx = lru_b_x.reshape(n_even, 2, groups, MXU_DIM)
    lru_gate_b = jnp.concatenate([ba[:, 0], bx[:, 0], ba[:, 1], bx[:, 1]], axis=-1).reshape(n_even, groups, 1, 4 * MXU_DIM)
    lru_cb = lru_conv_b.reshape(n_even, 1, lru_w)

    w_hg = od_w_in[:, :, :5 * hg_w].reshape(n_odd, d, 5, HG_HEADS // 2, 2 * hg_dk).transpose(0, 3, 2, 1, 4).astype(BF16)
    w_ret = od_w_in[:, :, 5 * hg_w:].reshape(n_odd, d, 4, RET_HEADS, ret_dk).transpose(0, 3, 2, 1, 4).astype(BF16)
    hg_ng = hg_norm_g.reshape(n_odd, 1, hg_w)
    ret_ng = ret_norm_g.reshape(n_odd, 1, ret_w)
    lbs = _lower_bounds(hg_lb_logits).reshape(depth, 1, hg_w)

    def trunk(x3):
        bsz, seq, _ = x3.shape
        n = bsz * seq
        cos, sin = _rope_tables(seq, ret_dk)
        x = x3.reshape(n, d)
        for l in range(depth):
            x, xb = _ffn_ln(x, fa_in, fa_out, l, lng, lnb, 0, alpha, True)
            xb3 = xb.reshape(bsz, seq, d)
            if l % 2 == 0:
                e = l // 2
                q, k, v, gc = _ml_prep(xb3, ev_in, e, ml_conv_w, ml_cb, wq_t, wk_t, wv_t, gw, gb_col)
                cf, cb = _ml_gates(gc)
                y1 = _ml_core(q, k, v, gc, cf, cb, xb3, ev_in, e, ml_ng)
                y2 = _deinterleave(_lru(_interleave(xb3, LRU_SEGS), ev_in, e, lru_conv_w, lru_cb, lru_gate,
                                        lru_gate_b, lru_lambda), LRU_SEGS)
                w_out, idx = ev_out, e
            else:
                o = l // 2
                y1 = _hgrn2(xb3, w_hg, o, lbs, l, hg_ng)
                y2 = _retention(xb3, w_ret, o, cos, sin, ret_ng)
                w_out, idx = od_out, o
            x = _proj_ln(x, y1.reshape(n, -1), y2.reshape(n, -1), w_out, idx, lng, lnb, l * 3 + 1, alpha)
            (x,) = _ffn_ln(x, fb_in, fb_out, l, lng, lnb, 2, alpha, False)
        return x.reshape(bsz, seq, d)

    return (trunk(x_prompt), trunk(x_sample))
```

```python
import functools
import math

import jax
import jax.numpy as jnp
from jax import lax
from jax.experimental import pallas as pl
from jax.experimental.pallas import tpu as pltpu

F32 = jnp.float32
BF16 = jnp.bfloat16

LN_EPS = 1e-5
LOG2E = 1.4426950408889634
CONV_W = 4
CONV_PAD_LEFT = 2
ML_HEADS = 4
ML_QKV_BLOCK = 4
LRU_BLOCKS = 16
LRU_C = 8.0
HG_HEADS = 8
RET_HEADS = 4
ROPE_BASE = 10000.0

LANES = 128
MXU_DIM = 256

TOKEN_TILE = 1024
FFN_CHUNK = 2816
FFN_ROW_GROUPS = 4
ML_CHUNK = MXU_DIM
RET_CHUNK = MXU_DIM
LRU_SEGS = 8
LRU_UNROLL = 8
LRU_TINY = 1e-37
LRU_SERIES_BELOW = 2.0 ** -9
ROW_BLOCK = MXU_DIM
CHUNK_UNROLL = 2
WIDE_UNROLL = 4
HG_CHUNK = LANES
HG_SAFE_DECAY = 64.0
VMEM_LIMIT = 56 * 1024 * 1024


def _mm(a, b):
    return jnp.dot(a, b, preferred_element_type=F32)


def _mm_nt(a, b):
    return lax.dot_general(a, b, (((1,), (1,)), ((), ())), preferred_element_type=F32)


def _mm_tn(a, b):
    return lax.dot_general(a, b, (((0,), (0,)), ((), ())), preferred_element_type=F32)


def _split3(x):
    x1 = x.astype(BF16)
    r = x - x1.astype(F32)
    x2 = r.astype(BF16)
    r = r - x2.astype(F32)
    return x1, x2, r.astype(BF16)


def _tri_left(tri, x):
    x1, x2, x3 = _split3(x)
    return _mm(tri, x1) + _mm(tri, x2) + _mm(tri, x3)


def _tri_right(x, tri):
    x1, x2, x3 = _split3(x)
    return _mm(x1, tri) + _mm(x2, tri) + _mm(x3, tri)


def _tri(n, lower):
    r = lax.broadcasted_iota(jnp.int32, (n, n), 0)
    c = lax.broadcasted_iota(jnp.int32, (n, n), 1)
    return r >= c if lower else r <= c


def _sigmoid(x):
    return 1.0 / (1.0 + jnp.exp(-x))


def _sigmoid_t(x):
    return 0.5 * jnp.tanh(0.5 * x) + 0.5


def _silu(x):
    return x * _sigmoid(x)


def _log_sigmoid(x):
    return jnp.minimum(x, 0.0) - jnp.log(1.0 + jnp.exp(-jnp.abs(x)))


def _layer_norm(y, g, b):
    mu = jnp.mean(y, axis=-1, keepdims=True)
    d = y - mu
    var = jnp.mean(d * d, axis=-1, keepdims=True)
    return d * lax.rsqrt(var + LN_EPS) * g + b


def _dwconv(u, w, b):
    n = u.shape[0]
    row = lax.broadcasted_iota(jnp.int32, u.shape, 0)
    acc = u * w[CONV_PAD_LEFT:CONV_PAD_LEFT + 1] + b
    for j in range(CONV_W):
        off = j - CONV_PAD_LEFT
        if off == 0:
            continue
        shifted = pltpu.roll(u, (-off) % n, 0)
        valid = (row + off >= 0) & (row + off < n)
        acc = acc + jnp.where(valid, shifted, 0.0) * w[j:j + 1]
    return acc


def _col(x, idx):
    lane = lax.broadcasted_iota(jnp.int32, x.shape, 1)
    return jnp.sum(jnp.where(lane == idx, x, 0.0), axis=1, keepdims=True)


def _row(x, idx):
    sub = lax.broadcasted_iota(jnp.int32, x.shape, 0)
    return jnp.sum(jnp.where(sub == idx, x, 0.0), axis=0, keepdims=True)


def _params(*sem):
    return pltpu.CompilerParams(dimension_semantics=sem, vmem_limit_bytes=VMEM_LIMIT)


def _lbs_kernel(logit_ref, o_ref):
    z = logit_ref[...]
    e = jnp.exp(z - jnp.max(z, axis=0, keepdims=True))
    p = e / jnp.sum(e, axis=0, keepdims=True)
    depth = z.shape[0]
    run = jnp.zeros_like(p[0:1])
    for l in range(1, depth):
        run = run + p[l:l + 1]
        o_ref[l:l + 1, :] = run
    o_ref[0:1, :] = jnp.zeros_like(run)


def _lower_bounds(logits):
    return pl.pallas_call(_lbs_kernel, out_shape=jax.ShapeDtypeStruct(logits.shape, F32))(logits.astype(F32))


def _ffn_kernel(x_ref, win_ref, wout_ref, g_ref, b_ref, *o_refs, d_ff, alpha):
    rows = x_ref.shape[0] // FFN_ROW_GROUPS
    for r0 in range(0, x_ref.shape[0], rows):
        x = x_ref[r0:r0 + rows, :]
        xb = x.astype(BF16)
        acc = jnp.zeros(x.shape, F32)
        for c0 in range(0, d_ff, FFN_CHUNK):
            gate = _mm(xb, win_ref[:, c0:c0 + FFN_CHUNK])
            up = _mm(xb, win_ref[:, d_ff + c0:d_ff + c0 + FFN_CHUNK])
            h = (_silu(gate) * up).astype(BF16)
            acc = acc + _mm(h, wout_ref[c0:c0 + FFN_CHUNK, :])
        y = _layer_norm(alpha * x + 0.5 * acc, g_ref[...], b_ref[...])
        o_refs[0][r0:r0 + rows, :] = y
        if len(o_refs) > 1:
            o_refs[1][r0:r0 + rows, :] = y.astype(BF16)


def _ffn_ln(x, w_in, w_out, layer, ln_g, ln_b, ln_idx, alpha, with_bf16):
    n, d = x.shape
    d_ff = w_out.shape[1]
    tm = min(TOKEN_TILE, n)
    assert n % tm == 0 and d_ff % FFN_CHUNK == 0
    out_shape = [jax.ShapeDtypeStruct((n, d), F32)]
    out_specs = [pl.BlockSpec((tm, d), lambda i: (i, 0))]
    if with_bf16:
        out_shape.append(jax.ShapeDtypeStruct((n, d), BF16))
        out_specs.append(pl.BlockSpec((tm, d), lambda i: (i, 0)))
    const = dict(pipeline_mode=pl.Buffered(1))
    return pl.pallas_call(
        functools.partial(_ffn_kernel, d_ff=d_ff, alpha=alpha),
        grid=(n // tm,),
        in_specs=[
            pl.BlockSpec((tm, d), lambda i: (i, 0)),
            pl.BlockSpec((None, d, 2 * d_ff), lambda i: (layer, 0, 0), **const),
            pl.BlockSpec((None, d_ff, d), lambda i: (layer, 0, 0), **const),
            pl.BlockSpec((None, 1, d), lambda i: (layer * 3 + ln_idx, 0, 0)),
            pl.BlockSpec((None, 1, d), lambda i: (layer * 3 + ln_idx, 0, 0)),
        ],
        out_specs=out_specs,
        out_shape=out_shape,
        compiler_params=_params("parallel"),
        name="ffn_ln",
    )(x, w_in, w_out, ln_g, ln_b)


def _proj_ln_kernel(x_ref, y1_ref, y2_ref, w_ref, g_ref, b_ref, o_ref, *, alpha):
    w1 = y1_ref.shape[1]
    rows = x_ref.shape[0] // FFN_ROW_GROUPS
    for r0 in range(0, x_ref.shape[0], rows):
        rs = slice(r0, r0 + rows)
        acc = _mm(y1_ref[rs, :], w_ref[0:w1, :]) + _mm(y2_ref[rs, :], w_ref[w1:, :])
        o_ref[rs, :] = _layer_norm(alpha * x_ref[rs, :] + acc, g_ref[...], b_ref[...])


def _proj_ln(x, y1, y2, w_out, idx, ln_g, ln_b, ln_row, alpha):
    n, d = x.shape
    w1, w2 = y1.shape[1], y2.shape[1]
    tm = min(TOKEN_TILE, n)
    return pl.pallas_call(
        functools.partial(_proj_ln_kernel, alpha=alpha),
        grid=(n // tm,),
        in_specs=[
            pl.BlockSpec((tm, d), lambda i: (i, 0)),
            pl.BlockSpec((tm, w1), lambda i: (i, 0)),
            pl.BlockSpec((tm, w2), lambda i: (i, 0)),
            pl.BlockSpec((None, w1 + w2, d), lambda i: (idx, 0, 0)),
            pl.BlockSpec((None, 1, d), lambda i: (ln_row, 0, 0)),
            pl.BlockSpec((None, 1, d), lambda i: (ln_row, 0, 0)),
        ],
        out_specs=pl.BlockSpec((tm, d), lambda i: (i, 0)),
        out_shape=jax.ShapeDtypeStruct((n, d), F32),
        compiler_params=_params("parallel"),
        name="proj_ln",
    )(x, y1, y2, w_out, ln_g, ln_b)


def _ml_prep_kernel(xb_ref, wu_ref, cw_ref, cb_ref, wq_ref, wk_ref, wv_ref, gw_ref, bcol_ref,
                    q_ref, k_ref, v_ref, gc_ref, c_ref, u_ref):
    u = _mm(xb_ref[...], wu_ref[...])
    c_ref[...] = _silu(_dwconv(u, cw_ref[...], cb_ref[...])).astype(BF16)
    u_ref[...] = u.astype(BF16)

    @pl.when(pl.program_id(1) == 0)
    def _():
        gc_ref[...] = jnp.broadcast_to(bcol_ref[...], gc_ref.shape)

    def block(i, carry):
        rows = pl.ds(pl.multiple_of(i * ROW_BLOCK, ROW_BLOCK), ROW_BLOCK)
        c = c_ref[rows, :]
        qb = _mm(c, wq_ref[...]).astype(BF16)
        kb = _mm(c, wk_ref[...]).astype(BF16)
        vb = _mm(u_ref[rows, :], wv_ref[...]).astype(BF16)
        q_ref[rows, :] = qb
        k_ref[rows, :] = kb
        v_ref[rows, :] = vb
        gc_ref[rows, :] += _mm(qb, gw_ref[0]) + _mm(kb, gw_ref[1]) + _mm(vb, gw_ref[2])
        return carry

    lax.fori_loop(0, q_ref.shape[0] // ROW_BLOCK, block, 0, unroll=CHUNK_UNROLL)


def _ml_prep(xb, w_in, e, cw, cb, wq, wk, wv, gw, bcol):
    bsz, seq, d = xb.shape
    dh = wq.shape[-1]
    heads = ML_HEADS
    qkv = jax.ShapeDtypeStruct((bsz, seq, heads * dh), BF16)
    head_blk = pl.BlockSpec((None, seq, dh), lambda b, h: (b, 0, h))
    return pl.pallas_call(
        _ml_prep_kernel,
        grid=(bsz, heads),
        in_specs=[
            pl.BlockSpec((None, seq, d), lambda b, h: (b, 0, 0)),
            pl.BlockSpec((None, d, dh), lambda b, h: (e, 0, h)),
            pl.BlockSpec((None, CONV_W, dh), lambda b, h: (e, 0, h)),
            pl.BlockSpec((None, 1, dh), lambda b, h: (e, 0, h)),
            pl.BlockSpec((None, None, dh, dh), lambda b, h: (e, h, 0, 0)),
            pl.BlockSpec((None, None, dh, dh), lambda b, h: (e, h, 0, 0)),
            pl.BlockSpec((None, None, dh, dh), lambda b, h: (e, h, 0, 0)),
            pl.BlockSpec((None, None, 3, dh, LANES), lambda b, h: (e, h, 0, 0, 0)),
            pl.BlockSpec((None, 1, LANES), lambda b, h: (e, 0, 0)),
        ],
        out_specs=[head_blk, head_blk, head_blk,
                   pl.BlockSpec((None, seq, LANES), lambda b, h: (b, 0, 0))],
        out_shape=[qkv, qkv, qkv, jax.ShapeDtypeStruct((bsz, seq, LANES), F32)],
        scratch_shapes=[pltpu.VMEM((seq, dh), BF16), pltpu.VMEM((seq, dh), BF16)],
        compiler_params=_params("parallel", "arbitrary"),
        name="mlstm_prep",
    )(xb, w_in, cw, cb, wq, wk, wv, gw, bcol)


def _ml_gates_kernel(gc_ref, cf_ref, cb_ref):
    ck = ML_CHUNK
    tri_l_b = jnp.where(_tri(ck, True), 1.0, 0.0).astype(BF16)
    tri_u_b = jnp.where(_tri(ck, False), 1.0, 0.0).astype(BF16)

    def step(c, carry):
        rows = pl.ds(pl.multiple_of(c * ck, ck), ck)
        lf = _log_sigmoid(gc_ref[rows, :]) * LOG2E
        cf_ref[rows, :] = _tri_left(tri_l_b, lf)
        cb_ref[rows, :] = _tri_left(tri_u_b, lf)
        return carry

    lax.fori_loop(0, gc_ref.shape[0] // ck, step, 0)


def _ml_gates(gc):
    bsz, seq, w = gc.shape
    blk = pl.BlockSpec((None, seq, w), lambda b: (b, 0, 0))
    out = jax.ShapeDtypeStruct(gc.shape, F32)
    return pl.pallas_call(
        _ml_gates_kernel, grid=(bsz,), in_specs=[blk], out_specs=[blk, blk], out_shape=[out, out],
        compiler_params=_params("parallel"), name="mlstm_gates",
    )(gc)


def _ml_core_kernel(q_ref, k_ref, v_ref, gc_ref, cf_ref, cb_ref, gr_ref, cfr_ref, cbr_ref, xb_ref, wz_ref, ng_ref,
                    y_ref, h_ref, col_ref, c_ref, n_ref, neg_ref):
    head = pl.program_id(1)
    seq, dh = q_ref.shape
    ck = ML_CHUNK
    nchunks = seq // ck
    wide = lambda x: jnp.concatenate([x] * (dh // LANES), axis=1)
    cum_src = (cf_ref, cb_ref)
    cum_row_src = (cfr_ref, cbr_ref)
    for dirn in range(2):
        col_ref[dirn] = jnp.broadcast_to(_col(cum_src[dirn][...], head * 4 + 2 + dirn), (seq, LANES))
        col_ref[2 + dirn] = jnp.broadcast_to(_col(gc_ref[...], head * 4 + dirn) * LOG2E, (seq, LANES))
        neg_ref[dirn] = jnp.where(_tri(ck, dirn == 0), 0.0, -jnp.inf)
    c_ref[...] = jnp.zeros(c_ref.shape, F32)
    n_ref[...] = jnp.zeros(n_ref.shape, F32)
    h_ref[...] = jnp.zeros(h_ref.shape, F32)

    def step(it, m_state):
        m_out = []
        for dirn in range(2):
            backward = dirn == 1
            m_prev = m_state[dirn]
            j = (nchunks - 1 - it) if backward else it
            rows = pl.ds(pl.multiple_of(j * ck, ck), ck)
            qc = q_ref[rows, :]
            kc = k_ref[rows, :]
            vc = v_ref[rows, :]
            cum_col = col_ref[dirn, rows, :]
            i_col = col_ref[2 + dirn, rows, :]
            cum_row = _row(cum_row_src[dirn][j], head * 4 + 2 + dirn)
            i_row = _row(gr_ref[j], head * 4 + dirn) * LOG2E
            far = 0 if backward else ck - 1
            total = cum_row[:, far:far + 1]

            d_log = wide(cum_col) + (i_row - cum_row) + neg_ref[dirn]
            inter_log = cum_col + m_prev
            m_t = jnp.maximum(inter_log, jnp.max(d_log, axis=1, keepdims=True))
            p = _mm_nt(qc, kc) * jnp.exp2(d_log - wide(m_t))
            w_inter = jnp.exp2(inter_log - m_t)
            num = _mm(p.astype(BF16), vc) + wide(w_inter) * _mm(qc, c_ref[dirn].astype(BF16))
            qn = jnp.sum(qc.astype(F32) * n_ref[dirn], axis=1, keepdims=True)
            den = jnp.sum(p, axis=1, keepdims=True) + w_inter * qn
            h_ref[rows, :] += num / wide(jnp.maximum(jnp.abs(den), jnp.exp2(-m_t)))

            g_col = total - cum_col + i_col
            g_row = total - cum_row + i_row
            m_new = jnp.maximum(total + m_prev, jnp.max(g_row, axis=1, keepdims=True))
            kw = kc.astype(F32) * wide(jnp.exp2(g_col - m_new))
            decay = jnp.exp2(total + m_prev - m_new)
            c_ref[dirn] = decay * c_ref[dirn] + _mm_tn(kw.astype(BF16), vc)
            n_ref[dirn] = decay * n_ref[dirn] + jnp.sum(kw, axis=0, keepdims=True)
            m_out.append(m_new)
        return tuple(m_out)

    lax.fori_loop(0, nchunks, step, (jnp.zeros((1, 1), F32), jnp.zeros((1, 1), F32)), unroll=CHUNK_UNROLL)
    h = h_ref[...]
    h = h - jnp.mean(h, axis=-1, keepdims=True)
    h = h * lax.rsqrt(jnp.mean(h * h, axis=-1, keepdims=True) + LN_EPS) * ng_ref[...]
    z = _mm(xb_ref[...], wz_ref[...])
    y_ref[...] = (_sigmoid_t(z) * h).astype(BF16)


def _ml_core(q, k, v, gc, cf, cb, xb, w_in, e, norm_g):
    bsz, seq, width = q.shape
    d = xb.shape[-1]
    heads = ML_HEADS
    dh = width // heads
    nchunks = seq // ML_CHUNK
    ng = heads * 4

    def rows(a):
        return a[:, :, :ng].reshape(bsz, nchunks, ML_CHUNK, ng).transpose(0, 1, 3, 2)

    head_blk = pl.BlockSpec((None, seq, dh), lambda b, h: (b, 0, h))
    col_blk = pl.BlockSpec((None, seq, LANES), lambda b, h: (b, 0, 0))
    row_blk = pl.BlockSpec((None, nchunks, ng, ML_CHUNK), lambda b, h: (b, 0, 0, 0))
    return pl.pallas_call(
        _ml_core_kernel,
        grid=(bsz, heads),
        in_specs=[
            head_blk, head_blk, head_blk,
            col_blk, col_blk, col_blk,
            row_blk, row_blk, row_blk,
            pl.BlockSpec((None, seq, d), lambda b, h: (b, 0, 0)),
            pl.BlockSpec((None, d, dh), lambda b, h: (e, 0, heads + h)),
            pl.BlockSpec((None, 1, dh), lambda b, h: (e, 0, h)),
        ],
        out_specs=head_blk,
        out_shape=jax.ShapeDtypeStruct((bsz, seq, width), BF16),
        scratch_shapes=[pltpu.VMEM((seq, dh), F32), pltpu.VMEM((4, seq, LANES), F32),
                        pltpu.VMEM((2, dh, dh), F32), pltpu.VMEM((2, 1, dh), F32),
                        pltpu.VMEM((2, ML_CHUNK, ML_CHUNK), F32)],
        compiler_params=_params("parallel", "arbitrary"),
        name="mlstm_core",
    )(q, k, v, gc, cf, cb, rows(gc), rows(cf), rows(cb), xb, w_in, norm_g)


def _dwconv_interleaved(u, w, b, nseg):
    n = u.shape[0]
    acc = u * w[CONV_PAD_LEFT:CONV_PAD_LEFT + 1] + b
    for j in range(CONV_W):
        off = j - CONV_PAD_LEFT
        if off == 0:
            continue
        k = nseg * abs(off)
        seg = lax.broadcasted_iota(jnp.int32, (k, u.shape[1]), 0) & (nseg - 1)
        if off < 0:
            edge = jnp.where(seg != 0, pltpu.roll(u[n - k:], 1, 0), 0.0)
            shifted = jnp.concatenate([edge, u[:n - k]], axis=0)
        else:
            edge = jnp.where(seg != nseg - 1, pltpu.roll(u[:k], k - 1, 0), 0.0)
            shifted = jnp.concatenate([u[k:], edge], axis=0)
        acc = acc + shifted * w[j:j + 1]
    return acc


def _lru_kernel(xb_ref, wu_ref, wg_ref, cw_ref, cb_ref, wgate_ref, bgate_ref, lam_ref, y_ref, a_ref, b_ref, x_ref):
    _, n, w = a_ref.shape
    nseg = LRU_SEGS
    steps = n // nseg
    x_ref[...] = _dwconv_interleaved(_mm(xb_ref[...], wu_ref[...]), cw_ref[...], cb_ref[...], nseg)
    lam = lam_ref[...]
    sp = jnp.maximum(-lam, 0.0) + jnp.log1p(jnp.exp(-jnp.abs(lam)))
    rate = (-0.5 * LRU_C * LOG2E) * sp
    ln2 = math.log(2.0)

    def gates(blk, carry):
        rows = pl.ds(pl.multiple_of(blk * ROW_BLOCK, ROW_BLOCK), ROW_BLOCK)
        x = x_ref[rows, :]
        half_x = 0.5 * x
        pre = _mm(x.astype(BF16), wgate_ref[...]) + bgate_ref[...]
        for dirn in range(2):
            t_a = jnp.tanh(pre[:, (2 * dirn) * w:(2 * dirn + 1) * w])
            t_i = jnp.tanh(pre[:, (2 * dirn + 1) * w:(2 * dirn + 2) * w])
            z = rate[dirn:dirn + 1] * (t_a + 1.0)
            a = jnp.exp2(z)
            a_ref[dirn, rows, :] = a
            series = z * ((z * (-4.0 / 3.0 * ln2 ** 3) - 2.0 * ln2 ** 2) * z - 2.0 * ln2)
            v = jnp.where(z > -LRU_SERIES_BELOW, series, 1.0 - a * a)
            b_ref[dirn, rows, :] = (v * lax.rsqrt(jnp.maximum(v, LRU_TINY))) * ((t_i + 1.0) * half_x)
        return carry

    lax.fori_loop(0, n // ROW_BLOCK, gates, 0, unroll=CHUNK_UNROLL)

    def step(it, carry):
        out = []
        for dirn in range(2):
            h, p = carry[2 * dirn], carry[2 * dirn + 1]
            j = (steps - 1 - it) if dirn else it
            rows = pl.ds(pl.multiple_of(j * nseg, nseg), nseg)
            a = a_ref[dirn, rows, :]
            h = a * h + b_ref[dirn, rows, :]
            p = p * a
            b_ref[dirn, rows, :] = h
            a_ref[dirn, rows, :] = p
            out += [h, p]
        return tuple(out)

    zeros = jnp.zeros((nseg, w), F32)
    ones = jnp.ones((nseg, w), F32)
    h_f, p_f, h_b, p_b = lax.fori_loop(0, steps, step, (zeros, ones, zeros, ones), unroll=LRU_UNROLL)
    srow = lax.broadcasted_iota(jnp.int32, (nseg, w), 0)
    c_f, c_b = zeros, zeros
    for _ in range(nseg - 1):
        c_f = jnp.where(srow >= 1, pltpu.roll(h_f + p_f * c_f, 1, 0), 0.0)
        c_b = jnp.where(srow <= nseg - 2, pltpu.roll(h_b + p_b * c_b, nseg - 1, 0), 0.0)
    def finish(blk, carry):
        rows = pl.ds(pl.multiple_of(blk * ROW_BLOCK, ROW_BLOCK), ROW_BLOCK)
        split = lambda ref, d: ref[d, rows, :].reshape(ROW_BLOCK // nseg, nseg, w)
        h = split(b_ref, 0) + split(a_ref, 0) * c_f[None] + split(b_ref, 1) + split(a_ref, 1) * c_b[None]
        g = _mm(xb_ref[rows, :], wg_ref[...])
        y_ref[rows, :] = (h.reshape(ROW_BLOCK, w) * jax.nn.gelu(g, approximate=True)).astype(BF16)
        return carry

    lax.fori_loop(0, n // ROW_BLOCK, finish, 0, unroll=CHUNK_UNROLL)


def _lru(xb, w_in, e, cw, cb, wgate, bgate, lam):
    bsz, seq, d = xb.shape
    groups, w = wgate.shape[1], wgate.shape[2]
    ublk = 2 * ML_HEADS
    return pl.pallas_call(
        _lru_kernel,
        grid=(bsz, groups),
        in_specs=[
            pl.BlockSpec((None, seq, d), lambda b, j: (b, 0, 0)),
            pl.BlockSpec((None, d, w), lambda b, j: (e, 0, ublk + j)),
            pl.BlockSpec((None, d, w), lambda b, j: (e, 0, ublk + groups + j)),
            pl.BlockSpec((None, CONV_W, w), lambda b, j: (e, 0, j)),
            pl.BlockSpec((None, 1, w), lambda b, j: (e, 0, j)),
            pl.BlockSpec((None, None, w, 4 * w), lambda b, j: (e, j, 0, 0)),
            pl.BlockSpec((None, None, 1, 4 * w), lambda b, j: (e, j, 0, 0)),
            pl.BlockSpec((None, 2, w), lambda b, j: (e, 0, j)),
        ],
        out_specs=pl.BlockSpec((None, seq, w), lambda b, j: (b, 0, j)),
        out_shape=jax.ShapeDtypeStruct((bsz, seq, groups * w), BF16),
        scratch_shapes=[pltpu.VMEM((2, seq, w), F32), pltpu.VMEM((2, seq, w), F32), pltpu.VMEM((seq, w), F32)],
        compiler_params=_params("parallel", "arbitrary"),
        name="rglru",
    )(xb, w_in, w_in, cw, cb, wgate, bgate, lam)


def _interleave(x, nseg):
    bsz, seq, w = x.shape
    return x.reshape(bsz, nseg, seq // nseg, w).transpose(0, 2, 1, 3).reshape(bsz, seq, w)


def _deinterleave(x, nseg):
    bsz, seq, w = x.shape
    return x.reshape(bsz, seq // nseg, nseg, w).transpose(0, 2, 1, 3).reshape(bsz, seq, w)


def _seg_bounds(cum, h, backward):
    n, w = cum.shape
    zero = jnp.zeros((1, w), F32)
    his, los = [], []
    for s0 in range(0, n, h):
        e0 = s0 + h - 1
        if backward:
            hi = cum[e0 + 1:e0 + 2] if e0 + 1 < n else zero
            lo = cum[s0:s0 + 1]
        else:
            hi = cum[s0 - 1:s0] if s0 > 0 else zero
            lo = cum[e0:e0 + 1]
        his.append(jnp.broadcast_to(hi, (h, w)))
        los.append(jnp.broadcast_to(lo, (h, w)))
    if len(his) == 1:
        return his[0], los[0]
    return jnp.concatenate(his, axis=0), jnp.concatenate(los, axis=0)


def _pair_rhs(x, dk):
    lane = lax.broadcasted_iota(jnp.int32, x.shape, 1)
    z = jnp.zeros_like(x)
    return jnp.concatenate([jnp.where(lane < dk, x, z), jnp.where(lane >= dk, x, z)], axis=0)


def _hg_variant(d32, d64, d128):
    use128 = d128 < HG_SAFE_DECAY
    use64 = jnp.logical_and(jnp.logical_not(use128), d64 < HG_SAFE_DECAY)
    use32 = jnp.logical_and(jnp.logical_not(jnp.logical_or(use128, use64)), d32 < HG_SAFE_DECAY)
    use1 = jnp.logical_not(jnp.logical_or(jnp.logical_or(use128, use64), use32))
    return {128: use128, 64: use64, 32: use32, 1: use1}


def _hg_kernel(xb_ref, w_ref, lb_ref, ng_ref, y_ref, q_ref, k_ref, g_ref, cum_ref, v_ref, o_ref, s_ref, gate_ref):
    seq, dk2 = q_ref.shape
    dk = dk2 // 2
    ck = HG_CHUNK
    nchunks = seq // ck
    lb = lb_ref[...]

    def project(i, carry):
        rows = pl.ds(pl.multiple_of(i * ROW_BLOCK, ROW_BLOCK), ROW_BLOCK)
        xb = xb_ref[rows, :]
        q_ref[rows, :] = _silu(_mm(xb, w_ref[0]))
        v_ref[rows, :] = _mm(xb, w_ref[3]).astype(BF16)
        for dirn in range(2):
            f = lb + (1.0 - lb) * _sigmoid(_mm(xb, w_ref[1 + dirn]))
            k_ref[dirn, rows, :] = 1.0 - f
            g_ref[dirn, rows, :] = jnp.log(f)
        gate_ref[rows, :] = _silu(_mm(xb, w_ref[4]))
        return carry

    lax.fori_loop(0, seq // ROW_BLOCK, project, 0, unroll=CHUNK_UNROLL)
    tri = (_tri(ck, True), _tri(ck, False))
    tri_b = tuple(jnp.where(m, 1.0, 0.0).astype(BF16) for m in tri)
    rr = lax.broadcasted_iota(jnp.int32, (ck, ck), 0)
    cc = lax.broadcasted_iota(jnp.int32, (ck, ck), 1)
    blk_xor = rr ^ cc
    trow = lax.broadcasted_iota(jnp.int32, (ck, dk2), 0)
    tlane = lax.broadcasted_iota(jnp.int32, (ck, dk2), 1)
    srow = lax.broadcasted_iota(jnp.int32, (dk2, dk2), 0)
    scol = lax.broadcasted_iota(jnp.int32, (dk2, dk2), 1)
    state_diag = (srow < dk) == (scol < dk)

    def cum_step(c, worst):
        r0 = pl.multiple_of(c * ck, ck)
        w32, w64, w128 = worst
        for dirn in range(2):
            cum = _tri_left(tri_b[dirn], g_ref[dirn, pl.ds(r0, ck), :])
            cum_ref[dirn, pl.ds(r0, ck), :] = cum
            ends = [ck - (i + 1) * 32 if dirn else (i + 1) * 32 - 1 for i in range(ck // 32)]
            c0, c1, c2, c3 = [cum[e:e + 1, :] for e in ends]
            w32 = jnp.maximum(w32, jnp.maximum(jnp.maximum(-c0, c0 - c1), jnp.maximum(c1 - c2, c2 - c3)))
            w64 = jnp.maximum(w64, jnp.maximum(-c1, c1 - c3))
            w128 = jnp.maximum(w128, -c3)
        return w32, w64, w128

    zero_row = jnp.zeros((1, dk2), F32)
    w32, w64, w128 = lax.fori_loop(0, nchunks, cum_step, (zero_row, zero_row, zero_row), unroll=WIDE_UNROLL)
    variant = _hg_variant(jnp.max(w32), jnp.max(w64), jnp.max(w128))

    def run(blk):
        s_ref[...] = jnp.zeros(s_ref.shape, F32)
        o_ref[...] = jnp.zeros(o_ref.shape, F32)
        big_levels = [h for h in (64, 32, 16, 8) if blk <= h < ck]
        small_levels = [h for h in (1, 2, 4) if h >= blk]
        pair = lambda m: jnp.concatenate([m, m], axis=1)
        in_blk = [pair(jnp.logical_and(tri[d], blk_xor < blk)) for d in range(2)] if blk > 1 else None
        same2h = {h: pair(blk_xor < 2 * h) for h in big_levels + small_levels}

        def add(att, part):
            return part if att is None else att + part

        def step(it, carry):
            for dirn in range(2):
                backward = dirn == 1
                j = (nchunks - 1 - it) if backward else it
                r0 = pl.multiple_of(j * ck, ck)
                qc = q_ref[pl.ds(r0, ck), :]
                kc = k_ref[dirn, pl.ds(r0, ck), :]
                cum = cum_ref[dirn, pl.ds(r0, ck), :]
                vc = v_ref[pl.ds(r0, ck), :]
                q_in = (qc * jnp.exp(cum)).astype(BF16)
                att = None
                if blk == ck:
                    kt = (kc * jnp.exp(-cum)).astype(BF16)
                    att = jnp.where(in_blk[dirn], _mm_nt(q_in, _pair_rhs(kt, dk)), 0.0)
                elif blk > 1:
                    ref, _ = _seg_bounds(cum, blk, backward)
                    qt = (qc * jnp.exp(cum - ref)).astype(BF16)
                    kt = (kc * jnp.exp(ref - cum)).astype(BF16)
                    att = jnp.where(in_blk[dirn], _mm_nt(qt, _pair_rhs(kt, dk)), 0.0)
                for h in big_levels:
                    seg_hi, seg_lo = _seg_bounds(cum, h, backward)
                    q_side = ((trow & h) != 0) != backward
                    qt = jnp.where(q_side, qc * jnp.exp(cum - seg_hi), 0.0).astype(BF16)
                    kt = jnp.where(q_side, 0.0, kc * jnp.exp(seg_lo - cum)).astype(BF16)
                    att = add(att, jnp.where(same2h[h], _mm_nt(qt, _pair_rhs(kt, dk)), 0.0))
                if small_levels:
                    seg_hi = cum - g_ref[dirn, pl.ds(r0, ck), :]
                    seg_lo = cum
                    for h in small_levels:
                        if h > 1:
                            bit = (trow & (h // 2)) != 0
                            seg_hi = jnp.maximum(seg_hi, jnp.where(bit, pltpu.roll(seg_hi, h // 2, 0),
                                                                   pltpu.roll(seg_hi, ck - h // 2, 0)))
                            seg_lo = jnp.minimum(seg_lo, jnp.where(bit, pltpu.roll(seg_lo, h // 2, 0),
                                                                   pltpu.roll(seg_lo, ck - h // 2, 0)))
                        q_side = ((trow & h) != 0) != backward
                        qt = jnp.where(q_side, qc * jnp.exp(cum - seg_hi), 0.0).astype(BF16)
                        kt = jnp.where(q_side, 0.0, kc * jnp.exp(seg_lo - cum)).astype(BF16)
                        att = add(att, jnp.where(same2h[h], _mm_nt(qt, _pair_rhs(kt, dk)), 0.0))
                oc = _mm(att.astype(BF16), _pair_rhs(vc, dk)) + _mm_nt(q_in, s_ref[dirn].astype(BF16))
                if blk == 1:
                    prod = qc * kc
                    da = jnp.sum(prod[:, :dk], axis=1, keepdims=True)
                    db = jnp.sum(prod[:, dk:], axis=1, keepdims=True)
                    oc = oc + jnp.where(tlane < dk, da, db) * vc.astype(F32)
                o_ref[pl.ds(r0, ck), :] += oc
                far = 0 if backward else ck - 1
                total = cum[far:far + 1, :]
                kd = (kc * jnp.exp(total - cum)).astype(BF16)
                s_ref[dirn] = jnp.where(state_diag, jnp.exp(total) * s_ref[dirn] + _mm_tn(vc, kd), 0.0)
            return carry

        lax.fori_loop(0, nchunks, step, 0, unroll=WIDE_UNROLL if blk > 1 else 1)

    for blk, use in variant.items():
        pl.when(use)(functools.partial(run, blk))

    o = o_ref[...]
    ng = ng_ref[...]
    halves = []
    for lo in (0, dk):
        oh = o[:, lo:lo + dk]
        halves.append(oh * lax.rsqrt(jnp.mean(oh * oh, axis=-1, keepdims=True) + LN_EPS) * ng[:, lo:lo + dk])
    y_ref[...] = (jnp.concatenate(halves, axis=1) * gate_ref[...]).astype(BF16)


def _hgrn2(xb, w_hg, o_idx, lbs, layer, norm_g):
    bsz, seq, d = xb.shape
    pairs, dk2 = w_hg.shape[1], w_hg.shape[-1]
    wide = lambda: pltpu.VMEM((seq, dk2), F32)
    both = lambda: pltpu.VMEM((2, seq, dk2), F32)
    return pl.pallas_call(
        _hg_kernel,
        grid=(bsz, pairs),
        in_specs=[
            pl.BlockSpec((None, seq, d), lambda b, h: (b, 0, 0)),
            pl.BlockSpec((None, None, 5, d, dk2), lambda b, h: (o_idx, h, 0, 0, 0)),
            pl.BlockSpec((None, 1, dk2), lambda b, h: (layer, 0, h)),
            pl.BlockSpec((None, 1, dk2), lambda b, h: (o_idx, 0, h)),
        ],
        out_specs=pl.BlockSpec((None, seq, dk2), lambda b, h: (b, 0, h)),
        out_shape=jax.ShapeDtypeStruct((bsz, seq, pairs * dk2), BF16),
        scratch_shapes=[wide(), both(), both(), both(), pltpu.VMEM((seq, dk2), BF16), wide(),
                        pltpu.VMEM((2, dk2, dk2), F32), wide()],
        compiler_params=_params("parallel", "arbitrary"),
        name="hgrn2",
    )(xb, w_hg, lbs, norm_g)


def _ret_kernel(xb_ref, w_ref, cos_ref, sin_ref, ng_ref, y_ref, q_ref, k_ref, v_ref, o_ref, s_ref, *, heads):
    head = pl.program_id(1)
    seq, dk = q_ref.shape
    ck = RET_CHUNK
    nchunks = seq // ck
    half = dk // 2
    xb = xb_ref[...]
    cos = cos_ref[...]
    sin = sin_ref[...]

    def rope(t):
        t1, t2 = t[:, :half], t[:, half:]
        return jnp.concatenate([t1 * cos - t2 * sin, t1 * sin + t2 * cos], axis=-1)

    q_ref[...] = rope(_mm(xb, w_ref[0])).astype(BF16)
    k_ref[...] = rope(_mm(xb, w_ref[1])) * (1.0 / math.sqrt(dk))
    v_ref[...] = _mm(xb, w_ref[2]).astype(BF16)

    hf = jnp.full((1, 1), head, jnp.int32).astype(F32)
    lg_f = jnp.log(1.0 - jnp.exp(-(5.0 + hf) * math.log(2.0)))
    lg_b = jnp.log(1.0 - jnp.exp(-(4.0 + heads - hf) * math.log(2.0)))
    rr = lax.broadcasted_iota(jnp.int32, (ck, ck), 0)
    cc = lax.broadcasted_iota(jnp.int32, (ck, ck), 1)
    diff = (rr - cc).astype(F32)
    intra = (jnp.where(diff >= 0, jnp.exp(jnp.maximum(diff, 0.0) * lg_f), 0.0)
             + jnp.where(diff <= 0, jnp.exp(jnp.maximum(-diff, 0.0) * lg_b), 0.0))
    pos_c = lax.broadcasted_iota(jnp.int32, (ck, 1), 0).astype(F32)

    lgs = (lg_f, lg_b)
    q_scale = (jnp.exp((pos_c + 1.0) * lg_f), jnp.exp((ck - pos_c) * lg_b))
    k_scale = (jnp.exp((ck - 1.0 - pos_c) * lg_f), jnp.exp(pos_c * lg_b))
    s_ref[...] = jnp.zeros(s_ref.shape, F32)
    o_ref[...] = jnp.zeros(o_ref.shape, F32)

    def step(it, carry):
        for dirn in range(2):
            j = (nchunks - 1 - it) if dirn else it
            rows = pl.ds(pl.multiple_of(j * ck, ck), ck)
            qc = q_ref[rows, :]
            kc = k_ref[rows, :]
            vc = v_ref[rows, :]
            oc = q_scale[dirn] * _mm(qc, s_ref[dirn].astype(BF16))
            if dirn == 0:
                att = _mm_nt(qc, kc.astype(BF16)) * intra
                oc = oc + _mm(att.astype(BF16), vc)
            o_ref[rows, :] += oc
            s_ref[dirn] = jnp.exp(ck * lgs[dirn]) * s_ref[dirn] + _mm_tn((kc * k_scale[dirn]).astype(BF16), vc)
        return carry

    lax.fori_loop(0, nchunks, step, 0, unroll=WIDE_UNROLL)
    o = o_ref[...]
    o = o - jnp.mean(o, axis=-1, keepdims=True)
    o = o * lax.rsqrt(jnp.mean(o * o, axis=-1, keepdims=True) + LN_EPS) * ng_ref[...]
    y_ref[...] = (o * _silu(_mm(xb, w_ref[3]))).astype(BF16)


def _retention(xb, w_ret, o_idx, cos, sin, norm_g):
    bsz, seq, d = xb.shape
    heads, dk = w_ret.shape[1], w_ret.shape[-1]
    return pl.pallas_call(
        functools.partial(_ret_kernel, heads=heads),
        grid=(bsz, heads),
        in_specs=[
            pl.BlockSpec((None, seq, d), lambda b, h: (b, 0, 0)),
            pl.BlockSpec((None, None, 4, d, dk), lambda b, h: (o_idx, h, 0, 0, 0)),
            pl.BlockSpec((seq, dk // 2), lambda b, h: (0, 0)),
            pl.BlockSpec((seq, dk // 2), lambda b, h: (0, 0)),
            pl.BlockSpec((None, 1, dk), lambda b, h: (o_idx, 0, h)),
        ],
        out_specs=pl.BlockSpec((None, seq, dk), lambda b, h: (b, 0, h)),
        out_shape=jax.ShapeDtypeStruct((bsz, seq, heads * dk), BF16),
        scratch_shapes=[pltpu.VMEM((seq, dk), BF16), pltpu.VMEM((seq, dk), F32), pltpu.VMEM((seq, dk), BF16),
                        pltpu.VMEM((seq, dk), F32), pltpu.VMEM((2, dk, dk), F32)],
        compiler_params=_params("parallel", "arbitrary"),
        name="retention",
    )(xb, w_ret, cos, sin, norm_g)


def _block_diag_tiles(w, tile):
    *lead, n, c, _ = w.shape
    per = tile // c
    w = w.reshape(*lead, n // per, per, c, c)
    eye = jnp.eye(per, dtype=w.dtype)
    t = jnp.einsum('...ncd,nm->...ncmd', w, eye)
    return t.reshape(*lead, n // per, tile, tile)


def _rope_tables(seq, dk):
    pos = jnp.arange(seq, dtype=F32)
    inv = ROPE_BASE ** (-jnp.arange(dk // 2, dtype=F32) / (dk // 2))
    ang = pos[:, None] * inv[None, :]
    return jnp.cos(ang), jnp.sin(ang)


def kernel(x_prompt, x_sample, ffn_a_w_in, ffn_a_w_out, ffn_b_w_in, ffn_b_w_out, ln_g, ln_b, ev_w_in, ev_w_out, ml_conv_w, ml_conv_b, ml_wq, ml_wk, ml_wv, ml_w_ig, ml_b_ig, ml_w_fg, ml_b_fg, ml_norm_g, lru_conv_w, lru_conv_b, lru_w_a, lru_b_a, lru_w_x, lru_b_x, lru_lambda, od_w_in, od_w_out, hg_lb_logits, hg_norm_g, ret_norm_g):
    depth, _, d = ln_g.shape
    n_even, n_odd = ev_w_in.shape[0], od_w_in.shape[0]
    alpha = (2.0 * depth) ** 0.25
    ml_w = ml_conv_w.shape[-1]
    dh = ml_w // ML_HEADS
    lru_w = lru_conv_w.shape[-1]
    hg_w = hg_norm_g.shape[-1]
    ret_w = ret_norm_g.shape[-1]
    hg_dk = hg_w // HG_HEADS
    ret_dk = ret_w // RET_HEADS

    fa_in, fa_out = ffn_a_w_in.astype(BF16), ffn_a_w_out.astype(BF16)
    fb_in, fb_out = ffn_b_w_in.astype(BF16), ffn_b_w_out.astype(BF16)
    lng = ln_g.reshape(depth * 3, 1, d)
    lnb = ln_b.reshape(depth * 3, 1, d)
    ev_in, ev_out = ev_w_in.astype(BF16), ev_w_out.astype(BF16)
    od_out = od_w_out.astype(BF16)

    wq_t = _block_diag_tiles(ml_wq, dh).astype(BF16)
    wk_t = (_block_diag_tiles(ml_wk, dh) * (1.0 / math.sqrt(dh))).astype(BF16)
    wv_t = _block_diag_tiles(ml_wv, dh).astype(BF16)
    gw = jnp.stack([ml_w_ig[:, 0], ml_w_ig[:, 1], ml_w_fg[:, 0], ml_w_fg[:, 1]], axis=-1)
    ngate = ML_HEADS * 4
    gw = gw.reshape(n_even, 3, ML_HEADS, dh, ngate).transpose(0, 2, 1, 3, 4)
    gw = jnp.pad(gw, ((0, 0),) * 4 + ((0, LANES - ngate),)).astype(BF16)
    gb = jnp.stack([ml_b_ig[:, 0], ml_b_ig[:, 1], ml_b_fg[:, 0], ml_b_fg[:, 1]], axis=-1).reshape(n_even, ngate)
    gb_col = jnp.pad(gb, ((0, 0), (0, LANES - ngate))).reshape(n_even, 1, LANES)
    ml_cb = ml_conv_b.reshape(n_even, 1, ml_w)
    ml_ng = ml_norm_g.reshape(n_even, 1, ml_w)

    wa_t = _block_diag_tiles(lru_w_a, MXU_DIM)
    wx_t = _block_diag_tiles(lru_w_x, MXU_DIM)
    lru_gate = (0.5 * jnp.concatenate([wa_t[:, 0], wx_t[:, 0], wa_t[:, 1], wx_t[:, 1]], axis=-1)).astype(BF16)
    groups = lru_w // MXU_DIM
    ba = lru_b_a.reshape(n_even, 2, groups, MXU_DIM)
    bx = lru_b_x.reshape(n_even, 2, groups, MXU_DIM)
    lru_gate_b = 0.5 * jnp.concatenate([ba[:, 0], bx[:, 0], ba[:, 1], bx[:, 1]], axis=-1).reshape(n_even, groups, 1, 4 * MXU_DIM)
    lru_cb = lru_conv_b.reshape(n_even, 1, lru_w)

    w_hg = od_w_in[:, :, :5 * hg_w].reshape(n_odd, d, 5, HG_HEADS // 2, 2 * hg_dk).transpose(0, 3, 2, 1, 4).astype(BF16)
    w_ret = od_w_in[:, :, 5 * hg_w:].reshape(n_odd, d, 4, RET_HEADS, ret_dk).transpose(0, 3, 2, 1, 4).astype(BF16)
    hg_ng = hg_norm_g.reshape(n_odd, 1, hg_w)
    ret_ng = ret_norm_g.reshape(n_odd, 1, ret_w)
    lbs = _lower_bounds(hg_lb_logits).reshape(depth, 1, hg_w)

    def trunk(x3):
        bsz, seq, _ = x3.shape
        n = bsz * seq
        cos, sin = _rope_tables(seq, ret_dk)
        x = x3.reshape(n, d)
        for l in range(depth):
            x, xb = _ffn_ln(x, fa_in, fa_out, l, lng, lnb, 0, alpha, True)
            xb3 = xb.reshape(bsz, seq, d)
            if l % 2 == 0:
                e = l // 2
                q, k, v, gc = _ml_prep(xb3, ev_in, e, ml_conv_w, ml_cb, wq_t, wk_t, wv_t, gw, gb_col)
                cf, cb = _ml_gates(gc)
                y1 = _ml_core(q, k, v, gc, cf, cb, xb3, ev_in, e, ml_ng)
                y2 = _deinterleave(_lru(_interleave(xb3, LRU_SEGS), ev_in, e, lru_conv_w, lru_cb, lru_gate,
                                        lru_gate_b, lru_lambda), LRU_SEGS)
                w_out, idx = ev_out, e
            else:
                o = l // 2
                y1 = _hgrn2(xb3, w_hg, o, lbs, l, hg_ng)
                y2 = _retention(xb3, w_ret, o, cos, sin, ret_ng)
                w_out, idx = od_out, o
            x = _proj_ln(x, y1.reshape(n, -1), y2.reshape(n, -1), w_out, idx, lng, lnb, l * 3 + 1, alpha)
            (x,) = _ffn_ln(x, fb_in, fb_out, l, lng, lnb, 2, alpha, False)
        return x.reshape(bsz, seq, d)

    return (trunk(x_prompt), trunk(x_sample))
```

```python
import functools
import math

import jax
import jax.numpy as jnp
from jax import lax
from jax.experimental import pallas as pl
from jax.experimental.pallas import tpu as pltpu

F32 = jnp.float32
BF16 = jnp.bfloat16

LN_EPS = 1e-5
LOG2E = 1.4426950408889634
CONV_W = 4
CONV_PAD_LEFT = 2
ML_HEADS = 4
ML_QKV_BLOCK = 4
LRU_BLOCKS = 16
LRU_C = 8.0
HG_HEADS = 8
RET_HEADS = 4
ROPE_BASE = 10000.0

LANES = 128
MXU_DIM = 256

TOKEN_TILE = 1024
FFN_CHUNK = 2816
FFN_ROW_GROUPS = 4
ML_CHUNK = MXU_DIM
RET_CHUNK = MXU_DIM
LRU_SEGS = 8
LRU_UNROLL = 8
LRU_TINY = 1e-37
LRU_SERIES_BELOW = 2.0 ** -9
ROW_BLOCK = MXU_DIM
CHUNK_UNROLL = 2
WIDE_UNROLL = 4
HG_CHUNK = LANES
HG_SAFE_DECAY = 64.0
VMEM_LIMIT = 56 * 1024 * 1024


def _mm(a, b):
    return jnp.dot(a, b, preferred_element_type=F32)


def _mm_nt(a, b):
    return lax.dot_general(a, b, (((1,), (1,)), ((), ())), preferred_element_type=F32)


def _mm_tn(a, b):
    return lax.dot_general(a, b, (((0,), (0,)), ((), ())), preferred_element_type=F32)


def _split3(x):
    x1 = x.astype(BF16)
    r = x - x1.astype(F32)
    x2 = r.astype(BF16)
    r = r - x2.astype(F32)
    return x1, x2, r.astype(BF16)


def _tri_left(tri, x):
    x1, x2, x3 = _split3(x)
    return _mm(tri, x1) + _mm(tri, x2) + _mm(tri, x3)


def _tri_right(x, tri):
    x1, x2, x3 = _split3(x)
    return _mm(x1, tri) + _mm(x2, tri) + _mm(x3, tri)


def _tri(n, lower):
    r = lax.broadcasted_iota(jnp.int32, (n, n), 0)
    c = lax.broadcasted_iota(jnp.int32, (n, n), 1)
    return r >= c if lower else r <= c


def _sigmoid(x):
    return 1.0 / (1.0 + jnp.exp(-x))


def _sigmoid_t(x):
    return 0.5 * jnp.tanh(0.5 * x) + 0.5


def _silu(x):
    return x * _sigmoid(x)


def _silu_t(x):
    h = 0.5 * x
    return h + h * jnp.tanh(h)


def _log_sigmoid(x):
    return jnp.minimum(x, 0.0) - jnp.log(1.0 + jnp.exp(-jnp.abs(x)))


def _layer_norm(y, g, b):
    mu = jnp.mean(y, axis=-1, keepdims=True)
    d = y - mu
    var = jnp.mean(d * d, axis=-1, keepdims=True)
    return d * lax.rsqrt(var + LN_EPS) * g + b


def _dwconv(u, w, b):
    n = u.shape[0]
    row = lax.broadcasted_iota(jnp.int32, u.shape, 0)
    acc = u * w[CONV_PAD_LEFT:CONV_PAD_LEFT + 1] + b
    for j in range(CONV_W):
        off = j - CONV_PAD_LEFT
        if off == 0:
            continue
        shifted = pltpu.roll(u, (-off) % n, 0)
        valid = (row + off >= 0) & (row + off < n)
        acc = acc + jnp.where(valid, shifted, 0.0) * w[j:j + 1]
    return acc


def _col(x, idx):
    lane = lax.broadcasted_iota(jnp.int32, x.shape, 1)
    return jnp.sum(jnp.where(lane == idx, x, 0.0), axis=1, keepdims=True)


def _row(x, idx):
    sub = lax.broadcasted_iota(jnp.int32, x.shape, 0)
    return jnp.sum(jnp.where(sub == idx, x, 0.0), axis=0, keepdims=True)


def _params(*sem):
    return pltpu.CompilerParams(dimension_semantics=sem, vmem_limit_bytes=VMEM_LIMIT)


def _lbs_kernel(logit_ref, o_ref):
    z = logit_ref[...]
    e = jnp.exp(z - jnp.max(z, axis=0, keepdims=True))
    p = e / jnp.sum(e, axis=0, keepdims=True)
    depth = z.shape[0]
    run = jnp.zeros_like(p[0:1])
    for l in range(1, depth):
        run = run + p[l:l + 1]
        o_ref[l:l + 1, :] = run
    o_ref[0:1, :] = jnp.zeros_like(run)


def _lower_bounds(logits):
    return pl.pallas_call(_lbs_kernel, out_shape=jax.ShapeDtypeStruct(logits.shape, F32))(logits.astype(F32))


def _ffn_kernel(x_ref, win_ref, wout_ref, g_ref, b_ref, *o_refs, d_ff, alpha):
    rows = x_ref.shape[0] // FFN_ROW_GROUPS
    for r0 in range(0, x_ref.shape[0], rows):
        x = x_ref[r0:r0 + rows, :]
        xb = x.astype(BF16)
        acc = jnp.zeros(x.shape, F32)
        for c0 in range(0, d_ff, FFN_CHUNK):
            gate = _mm(xb, win_ref[:, c0:c0 + FFN_CHUNK])
            up = _mm(xb, win_ref[:, d_ff + c0:d_ff + c0 + FFN_CHUNK])
            h = (_silu(gate) * up).astype(BF16)
            acc = acc + _mm(h, wout_ref[c0:c0 + FFN_CHUNK, :])
        y = _layer_norm(alpha * x + 0.5 * acc, g_ref[...], b_ref[...])
        o_refs[0][r0:r0 + rows, :] = y
        if len(o_refs) > 1:
            o_refs[1][r0:r0 + rows, :] = y.astype(BF16)


def _ffn_ln(x, w_in, w_out, layer, ln_g, ln_b, ln_idx, alpha, with_bf16):
    n, d = x.shape
    d_ff = w_out.shape[1]
    tm = min(TOKEN_TILE, n)
    assert n % tm == 0 and d_ff % FFN_CHUNK == 0
    out_shape = [jax.ShapeDtypeStruct((n, d), F32)]
    out_specs = [pl.BlockSpec((tm, d), lambda i: (i, 0))]
    if with_bf16:
        out_shape.append(jax.ShapeDtypeStruct((n, d), BF16))
        out_specs.append(pl.BlockSpec((tm, d), lambda i: (i, 0)))
    const = dict(pipeline_mode=pl.Buffered(1))
    return pl.pallas_call(
        functools.partial(_ffn_kernel, d_ff=d_ff, alpha=alpha),
        grid=(n // tm,),
        in_specs=[
            pl.BlockSpec((tm, d), lambda i: (i, 0)),
            pl.BlockSpec((None, d, 2 * d_ff), lambda i: (layer, 0, 0), **const),
            pl.BlockSpec((None, d_ff, d), lambda i: (layer, 0, 0), **const),
            pl.BlockSpec((None, 1, d), lambda i: (layer * 3 + ln_idx, 0, 0)),
            pl.BlockSpec((None, 1, d), lambda i: (layer * 3 + ln_idx, 0, 0)),
        ],
        out_specs=out_specs,
        out_shape=out_shape,
        compiler_params=_params("parallel"),
        name="ffn_ln",
    )(x, w_in, w_out, ln_g, ln_b)


def _proj_ln_kernel(x_ref, y1_ref, y2_ref, w_ref, g_ref, b_ref, o_ref, *, alpha):
    w1 = y1_ref.shape[1]
    rows = x_ref.shape[0] // FFN_ROW_GROUPS
    for r0 in range(0, x_ref.shape[0], rows):
        rs = slice(r0, r0 + rows)
        acc = _mm(y1_ref[rs, :], w_ref[0:w1, :]) + _mm(y2_ref[rs, :], w_ref[w1:, :])
        o_ref[rs, :] = _layer_norm(alpha * x_ref[rs, :] + acc, g_ref[...], b_ref[...])


def _proj_ln(x, y1, y2, w_out, idx, ln_g, ln_b, ln_row, alpha):
    n, d = x.shape
    w1, w2 = y1.shape[1], y2.shape[1]
    tm = min(TOKEN_TILE, n)
    return pl.pallas_call(
        functools.partial(_proj_ln_kernel, alpha=alpha),
        grid=(n // tm,),
        in_specs=[
            pl.BlockSpec((tm, d), lambda i: (i, 0)),
            pl.BlockSpec((tm, w1), lambda i: (i, 0)),
            pl.BlockSpec((tm, w2), lambda i: (i, 0)),
            pl.BlockSpec((None, w1 + w2, d), lambda i: (idx, 0, 0)),
            pl.BlockSpec((None, 1, d), lambda i: (ln_row, 0, 0)),
            pl.BlockSpec((None, 1, d), lambda i: (ln_row, 0, 0)),
        ],
        out_specs=pl.BlockSpec((tm, d), lambda i: (i, 0)),
        out_shape=jax.ShapeDtypeStruct((n, d), F32),
        compiler_params=_params("parallel"),
        name="proj_ln",
    )(x, y1, y2, w_out, ln_g, ln_b)


def _ml_prep_kernel(xb_ref, wu_ref, cw_ref, cb_ref, wq_ref, wk_ref, wv_ref, gw_ref, bcol_ref,
                    q_ref, k_ref, v_ref, gc_ref, c_ref, u_ref):
    u = _mm(xb_ref[...], wu_ref[...])
    c_ref[...] = _silu_t(_dwconv(u, cw_ref[...], cb_ref[...])).astype(BF16)
    u_ref[...] = u.astype(BF16)

    @pl.when(pl.program_id(1) == 0)
    def _():
        gc_ref[...] = jnp.broadcast_to(bcol_ref[...], gc_ref.shape)

    def block(i, carry):
        rows = pl.ds(pl.multiple_of(i * ROW_BLOCK, ROW_BLOCK), ROW_BLOCK)
        c = c_ref[rows, :]
        qb = _mm(c, wq_ref[...]).astype(BF16)
        kb = _mm(c, wk_ref[...]).astype(BF16)
        vb = _mm(u_ref[rows, :], wv_ref[...]).astype(BF16)
        q_ref[rows, :] = qb
        k_ref[rows, :] = kb
        v_ref[rows, :] = vb
        gc_ref[rows, :] += _mm(qb, gw_ref[0]) + _mm(kb, gw_ref[1]) + _mm(vb, gw_ref[2])
        return carry

    lax.fori_loop(0, q_ref.shape[0] // ROW_BLOCK, block, 0, unroll=CHUNK_UNROLL)


def _ml_prep(xb, w_in, e, cw, cb, wq, wk, wv, gw, bcol):
    bsz, seq, d = xb.shape
    dh = wq.shape[-1]
    heads = ML_HEADS
    qkv = jax.ShapeDtypeStruct((bsz, seq, heads * dh), BF16)
    head_blk = pl.BlockSpec((None, seq, dh), lambda b, h: (b, 0, h))
    return pl.pallas_call(
        _ml_prep_kernel,
        grid=(bsz, heads),
        in_specs=[
            pl.BlockSpec((None, seq, d), lambda b, h: (b, 0, 0)),
            pl.BlockSpec((None, d, dh), lambda b, h: (e, 0, h)),
            pl.BlockSpec((None, CONV_W, dh), lambda b, h: (e, 0, h)),
            pl.BlockSpec((None, 1, dh), lambda b, h: (e, 0, h)),
            pl.BlockSpec((None, None, dh, dh), lambda b, h: (e, h, 0, 0)),
            pl.BlockSpec((None, None, dh, dh), lambda b, h: (e, h, 0, 0)),
            pl.BlockSpec((None, None, dh, dh), lambda b, h: (e, h, 0, 0)),
            pl.BlockSpec((None, None, 3, dh, LANES), lambda b, h: (e, h, 0, 0, 0)),
            pl.BlockSpec((None, 1, LANES), lambda b, h: (e, 0, 0)),
        ],
        out_specs=[head_blk, head_blk, head_blk,
                   pl.BlockSpec((None, seq, LANES), lambda b, h: (b, 0, 0))],
        out_shape=[qkv, qkv, qkv, jax.ShapeDtypeStruct((bsz, seq, LANES), F32)],
        scratch_shapes=[pltpu.VMEM((seq, dh), BF16), pltpu.VMEM((seq, dh), BF16)],
        compiler_params=_params("parallel", "arbitrary"),
        name="mlstm_prep",
    )(xb, w_in, cw, cb, wq, wk, wv, gw, bcol)


def _ml_gates_kernel(gc_ref, cf_ref, cb_ref):
    ck = ML_CHUNK
    tri_l_b = jnp.where(_tri(ck, True), 1.0, 0.0).astype(BF16)
    tri_u_b = jnp.where(_tri(ck, False), 1.0, 0.0).astype(BF16)

    def step(c, carry):
        rows = pl.ds(pl.multiple_of(c * ck, ck), ck)
        lf = _log_sigmoid(gc_ref[rows, :]) * LOG2E
        cf_ref[rows, :] = _tri_left(tri_l_b, lf)
        cb_ref[rows, :] = _tri_left(tri_u_b, lf)
        return carry

    lax.fori_loop(0, gc_ref.shape[0] // ck, step, 0)


def _ml_gates(gc):
    bsz, seq, w = gc.shape
    blk = pl.BlockSpec((None, seq, w), lambda b: (b, 0, 0))
    out = jax.ShapeDtypeStruct(gc.shape, F32)
    return pl.pallas_call(
        _ml_gates_kernel, grid=(bsz,), in_specs=[blk], out_specs=[blk, blk], out_shape=[out, out],
        compiler_params=_params("parallel"), name="mlstm_gates",
    )(gc)


def _ml_core_kernel(q_ref, k_ref, v_ref, gc_ref, cf_ref, cb_ref, gr_ref, cfr_ref, cbr_ref, xb_ref, wz_ref, ng_ref,
                    y_ref, h_ref, col_ref, c_ref, n_ref, neg_ref):
    head = pl.program_id(1)
    seq, dh = q_ref.shape
    ck = ML_CHUNK
    nchunks = seq // ck
    wide = lambda x: jnp.concatenate([x] * (dh // LANES), axis=1)
    cum_src = (cf_ref, cb_ref)
    cum_row_src = (cfr_ref, cbr_ref)
    for dirn in range(2):
        col_ref[dirn] = jnp.broadcast_to(_col(cum_src[dirn][...], head * 4 + 2 + dirn), (seq, LANES))
        col_ref[2 + dirn] = jnp.broadcast_to(_col(gc_ref[...], head * 4 + dirn) * LOG2E, (seq, LANES))
        neg_ref[dirn] = jnp.where(_tri(ck, dirn == 0), 0.0, -jnp.inf)
    c_ref[...] = jnp.zeros(c_ref.shape, F32)
    n_ref[...] = jnp.zeros(n_ref.shape, F32)
    h_ref[...] = jnp.zeros(h_ref.shape, F32)

    def step(it, m_state):
        m_out = []
        for dirn in range(2):
            backward = dirn == 1
            m_prev = m_state[dirn]
            j = (nchunks - 1 - it) if backward else it
            rows = pl.ds(pl.multiple_of(j * ck, ck), ck)
            qc = q_ref[rows, :]
            kc = k_ref[rows, :]
            vc = v_ref[rows, :]
            cum_col = col_ref[dirn, rows, :]
            i_col = col_ref[2 + dirn, rows, :]
            cum_row = _row(cum_row_src[dirn][j], head * 4 + 2 + dirn)
            i_row = _row(gr_ref[j], head * 4 + dirn) * LOG2E
            far = 0 if backward else ck - 1
            total = cum_row[:, far:far + 1]

            d_log = wide(cum_col) + (i_row - cum_row) + neg_ref[dirn]
            inter_log = cum_col + m_prev
            m_t = jnp.maximum(inter_log, jnp.max(d_log, axis=1, keepdims=True))
            p = _mm_nt(qc, kc) * jnp.exp2(d_log - wide(m_t))
            w_inter = jnp.exp2(inter_log - m_t)
            num = _mm(p.astype(BF16), vc) + wide(w_inter) * _mm(qc, c_ref[dirn].astype(BF16))
            qn = jnp.sum(qc.astype(F32) * n_ref[dirn], axis=1, keepdims=True)
            den = jnp.sum(p, axis=1, keepdims=True) + w_inter * qn
            h_ref[rows, :] += num / wide(jnp.maximum(jnp.abs(den), jnp.exp2(-m_t)))

            g_col = total - cum_col + i_col
            g_row = total - cum_row + i_row
            m_new = jnp.maximum(total + m_prev, jnp.max(g_row, axis=1, keepdims=True))
            kw = kc.astype(F32) * wide(jnp.exp2(g_col - m_new))
            decay = jnp.exp2(total + m_prev - m_new)
            c_ref[dirn] = decay * c_ref[dirn] + _mm_tn(kw.astype(BF16), vc)
            n_ref[dirn] = decay * n_ref[dirn] + jnp.sum(kw, axis=0, keepdims=True)
            m_out.append(m_new)
        return tuple(m_out)

    lax.fori_loop(0, nchunks, step, (jnp.zeros((1, 1), F32), jnp.zeros((1, 1), F32)), unroll=CHUNK_UNROLL)
    h = h_ref[...]
    h = h - jnp.mean(h, axis=-1, keepdims=True)
    h = h * lax.rsqrt(jnp.mean(h * h, axis=-1, keepdims=True) + LN_EPS) * ng_ref[...]
    z = _mm(xb_ref[...], wz_ref[...])
    y_ref[...] = (_sigmoid_t(z) * h).astype(BF16)


def _ml_core(q, k, v, gc, cf, cb, xb, w_in, e, norm_g):
    bsz, seq, width = q.shape
    d = xb.shape[-1]
    heads = ML_HEADS
    dh = width // heads
    nchunks = seq // ML_CHUNK
    ng = heads * 4

    def rows(a):
        return a[:, :, :ng].reshape(bsz, nchunks, ML_CHUNK, ng).transpose(0, 1, 3, 2)

    head_blk = pl.BlockSpec((None, seq, dh), lambda b, h: (b, 0, h))
    col_blk = pl.BlockSpec((None, seq, LANES), lambda b, h: (b, 0, 0))
    row_blk = pl.BlockSpec((None, nchunks, ng, ML_CHUNK), lambda b, h: (b, 0, 0, 0))
    return pl.pallas_call(
        _ml_core_kernel,
        grid=(bsz, heads),
        in_specs=[
            head_blk, head_blk, head_blk,
            col_blk, col_blk, col_blk,
            row_blk, row_blk, row_blk,
            pl.BlockSpec((None, seq, d), lambda b, h: (b, 0, 0)),
            pl.BlockSpec((None, d, dh), lambda b, h: (e, 0, heads + h)),
            pl.BlockSpec((None, 1, dh), lambda b, h: (e, 0, h)),
        ],
        out_specs=head_blk,
        out_shape=jax.ShapeDtypeStruct((bsz, seq, width), BF16),
        scratch_shapes=[pltpu.VMEM((seq, dh), F32), pltpu.VMEM((4, seq, LANES), F32),
                        pltpu.VMEM((2, dh, dh), F32), pltpu.VMEM((2, 1, dh), F32),
                        pltpu.VMEM((2, ML_CHUNK, ML_CHUNK), F32)],
        compiler_params=_params("parallel", "arbitrary"),
        name="mlstm_core",
    )(q, k, v, gc, cf, cb, rows(gc), rows(cf), rows(cb), xb, w_in, norm_g)


def _dwconv_interleaved(u, w, b, nseg):
    n = u.shape[0]
    acc = u * w[CONV_PAD_LEFT:CONV_PAD_LEFT + 1] + b
    for j in range(CONV_W):
        off = j - CONV_PAD_LEFT
        if off == 0:
            continue
        k = nseg * abs(off)
        seg = lax.broadcasted_iota(jnp.int32, (k, u.shape[1]), 0) & (nseg - 1)
        if off < 0:
            edge = jnp.where(seg != 0, pltpu.roll(u[n - k:], 1, 0), 0.0)
            shifted = jnp.concatenate([edge, u[:n - k]], axis=0)
        else:
            edge = jnp.where(seg != nseg - 1, pltpu.roll(u[:k], k - 1, 0), 0.0)
            shifted = jnp.concatenate([u[k:], edge], axis=0)
        acc = acc + shifted * w[j:j + 1]
    return acc


def _lru_kernel(xb_ref, wu_ref, wg_ref, cw_ref, cb_ref, wgate_ref, bgate_ref, lam_ref, y_ref, a_ref, b_ref, x_ref):
    _, n, w = a_ref.shape
    nseg = LRU_SEGS
    steps = n // nseg
    x_ref[...] = _dwconv_interleaved(_mm(xb_ref[...], wu_ref[...]), cw_ref[...], cb_ref[...], nseg)
    lam = lam_ref[...]
    sp = jnp.maximum(-lam, 0.0) + jnp.log1p(jnp.exp(-jnp.abs(lam)))
    rate = (-0.5 * LRU_C * LOG2E) * sp
    ln2 = math.log(2.0)

    def gates(blk, carry):
        rows = pl.ds(pl.multiple_of(blk * ROW_BLOCK, ROW_BLOCK), ROW_BLOCK)
        x = x_ref[rows, :]
        half_x = 0.5 * x
        pre = _mm(x.astype(BF16), wgate_ref[...]) + bgate_ref[...]
        for dirn in range(2):
            t_a = jnp.tanh(pre[:, (2 * dirn) * w:(2 * dirn + 1) * w])
            t_i = jnp.tanh(pre[:, (2 * dirn + 1) * w:(2 * dirn + 2) * w])
            z = rate[dirn:dirn + 1] * (t_a + 1.0)
            a = jnp.exp2(z)
            a_ref[dirn, rows, :] = a
            series = z * ((z * (-4.0 / 3.0 * ln2 ** 3) - 2.0 * ln2 ** 2) * z - 2.0 * ln2)
            v = jnp.where(z > -LRU_SERIES_BELOW, series, 1.0 - a * a)
            b_ref[dirn, rows, :] = (v * lax.rsqrt(jnp.maximum(v, LRU_TINY))) * ((t_i + 1.0) * half_x)
        return carry

    lax.fori_loop(0, n // ROW_BLOCK, gates, 0, unroll=CHUNK_UNROLL)

    def step(it, carry):
        out = []
        for dirn in range(2):
            h, p = carry[2 * dirn], carry[2 * dirn + 1]
            j = (steps - 1 - it) if dirn else it
            rows = pl.ds(pl.multiple_of(j * nseg, nseg), nseg)
            a = a_ref[dirn, rows, :]
            h = a * h + b_ref[dirn, rows, :]
            p = p * a
            b_ref[dirn, rows, :] = h
            a_ref[dirn, rows, :] = p
            out += [h, p]
        return tuple(out)

    zeros = jnp.zeros((nseg, w), F32)
    ones = jnp.ones((nseg, w), F32)
    h_f, p_f, h_b, p_b = lax.fori_loop(0, steps, step, (zeros, ones, zeros, ones), unroll=LRU_UNROLL)
    srow = lax.broadcasted_iota(jnp.int32, (nseg, w), 0)
    c_f, c_b = zeros, zeros
    for _ in range(nseg - 1):
        c_f = jnp.where(srow >= 1, pltpu.roll(h_f + p_f * c_f, 1, 0), 0.0)
        c_b = jnp.where(srow <= nseg - 2, pltpu.roll(h_b + p_b * c_b, nseg - 1, 0), 0.0)
    def finish(blk, carry):
        rows = pl.ds(pl.multiple_of(blk * ROW_BLOCK, ROW_BLOCK), ROW_BLOCK)
        split = lambda ref, d: ref[d, rows, :].reshape(ROW_BLOCK // nseg, nseg, w)
        h = split(b_ref, 0) + split(a_ref, 0) * c_f[None] + split(b_ref, 1) + split(a_ref, 1) * c_b[None]
        g = _mm(xb_ref[rows, :], wg_ref[...])
        y_ref[rows, :] = (h.reshape(ROW_BLOCK, w) * jax.nn.gelu(g, approximate=True)).astype(BF16)
        return carry

    lax.fori_loop(0, n // ROW_BLOCK, finish, 0, unroll=CHUNK_UNROLL)


def _lru(xb, w_in, e, cw, cb, wgate, bgate, lam):
    bsz, seq, d = xb.shape
    groups, w = wgate.shape[1], wgate.shape[2]
    ublk = 2 * ML_HEADS
    return pl.pallas_call(
        _lru_kernel,
        grid=(bsz, groups),
        in_specs=[
            pl.BlockSpec((None, seq, d), lambda b, j: (b, 0, 0)),
            pl.BlockSpec((None, d, w), lambda b, j: (e, 0, ublk + j)),
            pl.BlockSpec((None, d, w), lambda b, j: (e, 0, ublk + groups + j)),
            pl.BlockSpec((None, CONV_W, w), lambda b, j: (e, 0, j)),
            pl.BlockSpec((None, 1, w), lambda b, j: (e, 0, j)),
            pl.BlockSpec((None, None, w, 4 * w), lambda b, j: (e, j, 0, 0)),
            pl.BlockSpec((None, None, 1, 4 * w), lambda b, j: (e, j, 0, 0)),
            pl.BlockSpec((None, 2, w), lambda b, j: (e, 0, j)),
        ],
        out_specs=pl.BlockSpec((None, seq, w), lambda b, j: (b, 0, j)),
        out_shape=jax.ShapeDtypeStruct((bsz, seq, groups * w), BF16),
        scratch_shapes=[pltpu.VMEM((2, seq, w), F32), pltpu.VMEM((2, seq, w), F32), pltpu.VMEM((seq, w), F32)],
        compiler_params=_params("parallel", "arbitrary"),
        name="rglru",
    )(xb, w_in, w_in, cw, cb, wgate, bgate, lam)


def _interleave(x, nseg):
    bsz, seq, w = x.shape
    return x.reshape(bsz, nseg, seq // nseg, w).transpose(0, 2, 1, 3).reshape(bsz, seq, w)


def _deinterleave(x, nseg):
    bsz, seq, w = x.shape
    return x.reshape(bsz, seq // nseg, nseg, w).transpose(0, 2, 1, 3).reshape(bsz, seq, w)


def _seg_bounds(cum, h, backward):
    n, w = cum.shape
    zero = jnp.zeros((1, w), F32)
    his, los = [], []
    for s0 in range(0, n, h):
        e0 = s0 + h - 1
        if backward:
            hi = cum[e0 + 1:e0 + 2] if e0 + 1 < n else zero
            lo = cum[s0:s0 + 1]
        else:
            hi = cum[s0 - 1:s0] if s0 > 0 else zero
            lo = cum[e0:e0 + 1]
        his.append(jnp.broadcast_to(hi, (h, w)))
        los.append(jnp.broadcast_to(lo, (h, w)))
    if len(his) == 1:
        return his[0], los[0]
    return jnp.concatenate(his, axis=0), jnp.concatenate(los, axis=0)


def _pair_rhs(x, dk):
    lane = lax.broadcasted_iota(jnp.int32, x.shape, 1)
    z = jnp.zeros_like(x)
    return jnp.concatenate([jnp.where(lane < dk, x, z), jnp.where(lane >= dk, x, z)], axis=0)


def _hg_variant(d32, d64, d128):
    use128 = d128 < HG_SAFE_DECAY
    use64 = jnp.logical_and(jnp.logical_not(use128), d64 < HG_SAFE_DECAY)
    use32 = jnp.logical_and(jnp.logical_not(jnp.logical_or(use128, use64)), d32 < HG_SAFE_DECAY)
    use1 = jnp.logical_not(jnp.logical_or(jnp.logical_or(use128, use64), use32))
    return {128: use128, 64: use64, 32: use32, 1: use1}


def _hg_kernel(xb_ref, w_ref, lb_ref, ng_ref, y_ref, q_ref, k_ref, g_ref, cum_ref, v_ref, o_ref, s_ref, gate_ref):
    seq, dk2 = q_ref.shape
    dk = dk2 // 2
    ck = HG_CHUNK
    nchunks = seq // ck
    lb = lb_ref[...]

    def project(i, carry):
        rows = pl.ds(pl.multiple_of(i * ROW_BLOCK, ROW_BLOCK), ROW_BLOCK)
        xb = xb_ref[rows, :]
        q_ref[rows, :] = _silu_t(_mm(xb, w_ref[0]))
        v_ref[rows, :] = _mm(xb, w_ref[3]).astype(BF16)
        for dirn in range(2):
            f = lb + (1.0 - lb) * _sigmoid(_mm(xb, w_ref[1 + dirn]))
            k_ref[dirn, rows, :] = 1.0 - f
            g_ref[dirn, rows, :] = jnp.log(f)
        gate_ref[rows, :] = _silu_t(_mm(xb, w_ref[4]))
        return carry

    lax.fori_loop(0, seq // ROW_BLOCK, project, 0, unroll=WIDE_UNROLL)
    tri = (_tri(ck, True), _tri(ck, False))
    tri_b = tuple(jnp.where(m, 1.0, 0.0).astype(BF16) for m in tri)
    rr = lax.broadcasted_iota(jnp.int32, (ck, ck), 0)
    cc = lax.broadcasted_iota(jnp.int32, (ck, ck), 1)
    blk_xor = rr ^ cc
    trow = lax.broadcasted_iota(jnp.int32, (ck, dk2), 0)
    tlane = lax.broadcasted_iota(jnp.int32, (ck, dk2), 1)
    srow = lax.broadcasted_iota(jnp.int32, (dk2, dk2), 0)
    scol = lax.broadcasted_iota(jnp.int32, (dk2, dk2), 1)
    state_diag = (srow < dk) == (scol < dk)

    def cum_step(c, worst):
        r0 = pl.multiple_of(c * ck, ck)
        w32, w64, w128 = worst
        for dirn in range(2):
            cum = _tri_left(tri_b[dirn], g_ref[dirn, pl.ds(r0, ck), :])
            cum_ref[dirn, pl.ds(r0, ck), :] = cum
            ends = [ck - (i + 1) * 32 if dirn else (i + 1) * 32 - 1 for i in range(ck // 32)]
            c0, c1, c2, c3 = [cum[e:e + 1, :] for e in ends]
            w32 = jnp.maximum(w32, jnp.maximum(jnp.maximum(-c0, c0 - c1), jnp.maximum(c1 - c2, c2 - c3)))
            w64 = jnp.maximum(w64, jnp.maximum(-c1, c1 - c3))
            w128 = jnp.maximum(w128, -c3)
        return w32, w64, w128

    zero_row = jnp.zeros((1, dk2), F32)
    w32, w64, w128 = lax.fori_loop(0, nchunks, cum_step, (zero_row, zero_row, zero_row), unroll=WIDE_UNROLL)
    variant = _hg_variant(jnp.max(w32), jnp.max(w64), jnp.max(w128))

    def run(blk):
        s_ref[...] = jnp.zeros(s_ref.shape, F32)
        o_ref[...] = jnp.zeros(o_ref.shape, F32)
        big_levels = [h for h in (64, 32, 16, 8) if blk <= h < ck]
        small_levels = [h for h in (1, 2, 4) if h >= blk]
        pair = lambda m: jnp.concatenate([m, m], axis=1)
        in_blk = [pair(jnp.logical_and(tri[d], blk_xor < blk)) for d in range(2)] if blk > 1 else None
        same2h = {h: pair(blk_xor < 2 * h) for h in big_levels + small_levels}

        def add(att, part):
            return part if att is None else att + part

        def step(it, carry):
            for dirn in range(2):
                backward = dirn == 1
                j = (nchunks - 1 - it) if backward else it
                r0 = pl.multiple_of(j * ck, ck)
                qc = q_ref[pl.ds(r0, ck), :]
                kc = k_ref[dirn, pl.ds(r0, ck), :]
                cum = cum_ref[dirn, pl.ds(r0, ck), :]
                vc = v_ref[pl.ds(r0, ck), :]
                q_in = (qc * jnp.exp(cum)).astype(BF16)
                att = None
                if blk == ck:
                    kt = (kc * jnp.exp(-cum)).astype(BF16)
                    att = jnp.where(in_blk[dirn], _mm_nt(q_in, _pair_rhs(kt, dk)), 0.0)
                elif blk > 1:
                    ref, _ = _seg_bounds(cum, blk, backward)
                    qt = (qc * jnp.exp(cum - ref)).astype(BF16)
                    kt = (kc * jnp.exp(ref - cum)).astype(BF16)
                    att = jnp.where(in_blk[dirn], _mm_nt(qt, _pair_rhs(kt, dk)), 0.0)
                for h in big_levels:
                    seg_hi, seg_lo = _seg_bounds(cum, h, backward)
                    q_side = ((trow & h) != 0) != backward
                    qt = jnp.where(q_side, qc * jnp.exp(cum - seg_hi), 0.0).astype(BF16)
                    kt = jnp.where(q_side, 0.0, kc * jnp.exp(seg_lo - cum)).astype(BF16)
                    att = add(att, jnp.where(same2h[h], _mm_nt(qt, _pair_rhs(kt, dk)), 0.0))
                if small_levels:
                    seg_hi = cum - g_ref[dirn, pl.ds(r0, ck), :]
                    seg_lo = cum
                    for h in small_levels:
                        if h > 1:
                            bit = (trow & (h // 2)) != 0
                            seg_hi = jnp.maximum(seg_hi, jnp.where(bit, pltpu.roll(seg_hi, h // 2, 0),
                                                                   pltpu.roll(seg_hi, ck - h // 2, 0)))
                            seg_lo = jnp.minimum(seg_lo, jnp.where(bit, pltpu.roll(seg_lo, h // 2, 0),
                                                                   pltpu.roll(seg_lo, ck - h // 2, 0)))
                        q_side = ((trow & h) != 0) != backward
                        qt = jnp.where(q_side, qc * jnp.exp(cum - seg_hi), 0.0).astype(BF16)
                        kt = jnp.where(q_side, 0.0, kc * jnp.exp(seg_lo - cum)).astype(BF16)
                        att = add(att, jnp.where(same2h[h], _mm_nt(qt, _pair_rhs(kt, dk)), 0.0))
                oc = _mm(att.astype(BF16), _pair_rhs(vc, dk)) + _mm_nt(q_in, s_ref[dirn].astype(BF16))
                if blk == 1:
                    prod = qc * kc
                    da = jnp.sum(prod[:, :dk], axis=1, keepdims=True)
                    db = jnp.sum(prod[:, dk:], axis=1, keepdims=True)
                    oc = oc + jnp.where(tlane < dk, da, db) * vc.astype(F32)
                o_ref[pl.ds(r0, ck), :] += oc
                far = 0 if backward else ck - 1
                total = cum[far:far + 1, :]
                kd = (kc * jnp.exp(total - cum)).astype(BF16)
                s_ref[dirn] = jnp.where(state_diag, jnp.exp(total) * s_ref[dirn] + _mm_tn(vc, kd), 0.0)
            return carry

        lax.fori_loop(0, nchunks, step, 0, unroll=WIDE_UNROLL if blk > 1 else 1)

    for blk, use in variant.items():
        pl.when(use)(functools.partial(run, blk))

    o = o_ref[...]
    ng = ng_ref[...]
    halves = []
    for lo in (0, dk):
        oh = o[:, lo:lo + dk]
        halves.append(oh * lax.rsqrt(jnp.mean(oh * oh, axis=-1, keepdims=True) + LN_EPS) * ng[:, lo:lo + dk])
    y_ref[...] = (jnp.concatenate(halves, axis=1) * gate_ref[...]).astype(BF16)


def _hgrn2(xb, w_hg, o_idx, lbs, layer, norm_g):
    bsz, seq, d = xb.shape
    pairs, dk2 = w_hg.shape[1], w_hg.shape[-1]
    wide = lambda: pltpu.VMEM((seq, dk2), F32)
    both = lambda: pltpu.VMEM((2, seq, dk2), F32)
    return pl.pallas_call(
        _hg_kernel,
        grid=(bsz, pairs),
        in_specs=[
            pl.BlockSpec((None, seq, d), lambda b, h: (b, 0, 0)),
            pl.BlockSpec((None, None, 5, d, dk2), lambda b, h: (o_idx, h, 0, 0, 0)),
            pl.BlockSpec((None, 1, dk2), lambda b, h: (layer, 0, h)),
            pl.BlockSpec((None, 1, dk2), lambda b, h: (o_idx, 0, h)),
        ],
        out_specs=pl.BlockSpec((None, seq, dk2), lambda b, h: (b, 0, h)),
        out_shape=jax.ShapeDtypeStruct((bsz, seq, pairs * dk2), BF16),
        scratch_shapes=[wide(), both(), both(), both(), pltpu.VMEM((seq, dk2), BF16), wide(),
                        pltpu.VMEM((2, dk2, dk2), F32), wide()],
        compiler_params=_params("parallel", "arbitrary"),
        name="hgrn2",
    )(xb, w_hg, lbs, norm_g)


def _ret_kernel(xb_ref, w_ref, cos_ref, sin_ref, ng_ref, y_ref, q_ref, k_ref, v_ref, o_ref, s_ref, *, heads):
    head = pl.program_id(1)
    seq, dk = q_ref.shape
    ck = RET_CHUNK
    nchunks = seq // ck
    half = dk // 2
    xb = xb_ref[...]
    cos = cos_ref[...]
    sin = sin_ref[...]

    def rope(t):
        t1, t2 = t[:, :half], t[:, half:]
        return jnp.concatenate([t1 * cos - t2 * sin, t1 * sin + t2 * cos], axis=-1)

    q_ref[...] = rope(_mm(xb, w_ref[0])).astype(BF16)
    k_ref[...] = rope(_mm(xb, w_ref[1])) * (1.0 / math.sqrt(dk))
    v_ref[...] = _mm(xb, w_ref[2]).astype(BF16)

    hf = jnp.full((1, 1), head, jnp.int32).astype(F32)
    lg_f = jnp.log(1.0 - jnp.exp(-(5.0 + hf) * math.log(2.0)))
    lg_b = jnp.log(1.0 - jnp.exp(-(4.0 + heads - hf) * math.log(2.0)))
    rr = lax.broadcasted_iota(jnp.int32, (ck, ck), 0)
    cc = lax.broadcasted_iota(jnp.int32, (ck, ck), 1)
    diff = (rr - cc).astype(F32)
    intra = (jnp.where(diff >= 0, jnp.exp(jnp.maximum(diff, 0.0) * lg_f), 0.0)
             + jnp.where(diff <= 0, jnp.exp(jnp.maximum(-diff, 0.0) * lg_b), 0.0))
    pos_c = lax.broadcasted_iota(jnp.int32, (ck, 1), 0).astype(F32)

    lgs = (lg_f, lg_b)
    q_scale = (jnp.exp((pos_c + 1.0) * lg_f), jnp.exp((ck - pos_c) * lg_b))
    k_scale = (jnp.exp((ck - 1.0 - pos_c) * lg_f), jnp.exp(pos_c * lg_b))
    s_ref[...] = jnp.zeros(s_ref.shape, F32)
    o_ref[...] = jnp.zeros(o_ref.shape, F32)

    def step(it, carry):
        for dirn in range(2):
            j = (nchunks - 1 - it) if dirn else it
            rows = pl.ds(pl.multiple_of(j * ck, ck), ck)
            qc = q_ref[rows, :]
            kc = k_ref[rows, :]
            vc = v_ref[rows, :]
            oc = q_scale[dirn] * _mm(qc, s_ref[dirn].astype(BF16))
            if dirn == 0:
                att = _mm_nt(qc, kc.astype(BF16)) * intra
                oc = oc + _mm(att.astype(BF16), vc)
            o_ref[rows, :] += oc
            s_ref[dirn] = jnp.exp(ck * lgs[dirn]) * s_ref[dirn] + _mm_tn((kc * k_scale[dirn]).astype(BF16), vc)
        return carry

    lax.fori_loop(0, nchunks, step, 0, unroll=WIDE_UNROLL)
    o = o_ref[...]
    o = o - jnp.mean(o, axis=-1, keepdims=True)
    o = o * lax.rsqrt(jnp.mean(o * o, axis=-1, keepdims=True) + LN_EPS) * ng_ref[...]
    y_ref[...] = (o * _silu_t(_mm(xb, w_ref[3]))).astype(BF16)


def _retention(xb, w_ret, o_idx, cos, sin, norm_g):
    bsz, seq, d = xb.shape
    heads, dk = w_ret.shape[1], w_ret.shape[-1]
    return pl.pallas_call(
        functools.partial(_ret_kernel, heads=heads),
        grid=(bsz, heads),
        in_specs=[
            pl.BlockSpec((None, seq, d), lambda b, h: (b, 0, 0)),
            pl.BlockSpec((None, None, 4, d, dk), lambda b, h: (o_idx, h, 0, 0, 0)),
            pl.BlockSpec((seq, dk // 2), lambda b, h: (0, 0)),
            pl.BlockSpec((seq, dk // 2), lambda b, h: (0, 0)),
            pl.BlockSpec((None, 1, dk), lambda b, h: (o_idx, 0, h)),
        ],
        out_specs=pl.BlockSpec((None, seq, dk), lambda b, h: (b, 0, h)),
        out_shape=jax.ShapeDtypeStruct((bsz, seq, heads * dk), BF16),
        scratch_shapes=[pltpu.VMEM((seq, dk), BF16), pltpu.VMEM((seq, dk), F32), pltpu.VMEM((seq, dk), BF16),
                        pltpu.VMEM((seq, dk), F32), pltpu.VMEM((2, dk, dk), F32)],
        compiler_params=_params("parallel", "arbitrary"),
        name="retention",
    )(xb, w_ret, cos, sin, norm_g)


def _block_diag_tiles(w, tile):
    *lead, n, c, _ = w.shape
    per = tile // c
    w = w.reshape(*lead, n // per, per, c, c)
    eye = jnp.eye(per, dtype=w.dtype)
    t = jnp.einsum('...ncd,nm->...ncmd', w, eye)
    return t.reshape(*lead, n // per, tile, tile)


def _rope_tables(seq, dk):
    pos = jnp.arange(seq, dtype=F32)
    inv = ROPE_BASE ** (-jnp.arange(dk // 2, dtype=F32) / (dk // 2))
    ang = pos[:, None] * inv[None, :]
    return jnp.cos(ang), jnp.sin(ang)


def kernel(x_prompt, x_sample, ffn_a_w_in, ffn_a_w_out, ffn_b_w_in, ffn_b_w_out, ln_g, ln_b, ev_w_in, ev_w_out, ml_conv_w, ml_conv_b, ml_wq, ml_wk, ml_wv, ml_w_ig, ml_b_ig, ml_w_fg, ml_b_fg, ml_norm_g, lru_conv_w, lru_conv_b, lru_w_a, lru_b_a, lru_w_x, lru_b_x, lru_lambda, od_w_in, od_w_out, hg_lb_logits, hg_norm_g, ret_norm_g):
    depth, _, d = ln_g.shape
    n_even, n_odd = ev_w_in.shape[0], od_w_in.shape[0]
    alpha = (2.0 * depth) ** 0.25
    ml_w = ml_conv_w.shape[-1]
    dh = ml_w // ML_HEADS
    lru_w = lru_conv_w.shape[-1]
    hg_w = hg_norm_g.shape[-1]
    ret_w = ret_norm_g.shape[-1]
    hg_dk = hg_w // HG_HEADS
    ret_dk = ret_w // RET_HEADS

    fa_in, fa_out = ffn_a_w_in.astype(BF16), ffn_a_w_out.astype(BF16)
    fb_in, fb_out = ffn_b_w_in.astype(BF16), ffn_b_w_out.astype(BF16)
    lng = ln_g.reshape(depth * 3, 1, d)
    lnb = ln_b.reshape(depth * 3, 1, d)
    ev_in, ev_out = ev_w_in.astype(BF16), ev_w_out.astype(BF16)
    od_out = od_w_out.astype(BF16)

    wq_t = _block_diag_tiles(ml_wq, dh).astype(BF16)
    wk_t = (_block_diag_tiles(ml_wk, dh) * (1.0 / math.sqrt(dh))).astype(BF16)
    wv_t = _block_diag_tiles(ml_wv, dh).astype(BF16)
    gw = jnp.stack([ml_w_ig[:, 0], ml_w_ig[:, 1], ml_w_fg[:, 0], ml_w_fg[:, 1]], axis=-1)
    ngate = ML_HEADS * 4
    gw = gw.reshape(n_even, 3, ML_HEADS, dh, ngate).transpose(0, 2, 1, 3, 4)
    gw = jnp.pad(gw, ((0, 0),) * 4 + ((0, LANES - ngate),)).astype(BF16)
    gb = jnp.stack([ml_b_ig[:, 0], ml_b_ig[:, 1], ml_b_fg[:, 0], ml_b_fg[:, 1]], axis=-1).reshape(n_even, ngate)
    gb_col = jnp.pad(gb, ((0, 0), (0, LANES - ngate))).reshape(n_even, 1, LANES)
    ml_cb = ml_conv_b.reshape(n_even, 1, ml_w)
    ml_ng = ml_norm_g.reshape(n_even, 1, ml_w)

    wa_t = _block_diag_tiles(lru_w_a, MXU_DIM)
    wx_t = _block_diag_tiles(lru_w_x, MXU_DIM)
    lru_gate = (0.5 * jnp.concatenate([wa_t[:, 0], wx_t[:, 0], wa_t[:, 1], wx_t[:, 1]], axis=-1)).astype(BF16)
    groups = lru_w // MXU_DIM
    ba = lru_b_a.reshape(n_even, 2, groups, MXU_DIM)
    bx = lru_b_x.reshape(n_even, 2, groups, MXU_DIM)
    lru_gate_b = 0.5 * jnp.concatenate([ba[:, 0], bx[:, 0], ba[:, 1], bx[:, 1]], axis=-1).reshape(n_even, groups, 1, 4 * MXU_DIM)
    lru_cb = lru_conv_b.reshape(n_even, 1, lru_w)

    w_hg = od_w_in[:, :, :5 * hg_w].reshape(n_odd, d, 5, HG_HEADS // 2, 2 * hg_dk).transpose(0, 3, 2, 1, 4).astype(BF16)
    w_ret = od_w_in[:, :, 5 * hg_w:].reshape(n_odd, d, 4, RET_HEADS, ret_dk).transpose(0, 3, 2, 1, 4).astype(BF16)
    hg_ng = hg_norm_g.reshape(n_odd, 1, hg_w)
    ret_ng = ret_norm_g.reshape(n_odd, 1, ret_w)
    lbs = _lower_bounds(hg_lb_logits).reshape(depth, 1, hg_w)

    def trunk(x3):
        bsz, seq, _ = x3.shape
        n = bsz * seq
        cos, sin = _rope_tables(seq, ret_dk)
        x = x3.reshape(n, d)
        for l in range(depth):
            x, xb = _ffn_ln(x, fa_in, fa_out, l, lng, lnb, 0, alpha, True)
            xb3 = xb.reshape(bsz, seq, d)
            if l % 2 == 0:
                e = l // 2
                q, k, v, gc = _ml_prep(xb3, ev_in, e, ml_conv_w, ml_cb, wq_t, wk_t, wv_t, gw, gb_col)
                cf, cb = _ml_gates(gc)
                y1 = _ml_core(q, k, v, gc, cf, cb, xb3, ev_in, e, ml_ng)
                y2 = _deinterleave(_lru(_interleave(xb3, LRU_SEGS), ev_in, e, lru_conv_w, lru_cb, lru_gate,
                                        lru_gate_b, lru_lambda), LRU_SEGS)
                w_out, idx = ev_out, e
            else:
                o = l // 2
                y1 = _hgrn2(xb3, w_hg, o, lbs, l, hg_ng)
                y2 = _retention(xb3, w_ret, o, cos, sin, ret_ng)
                w_out, idx = od_out, o
            x = _proj_ln(x, y1.reshape(n, -1), y2.reshape(n, -1), w_out, idx, lng, lnb, l * 3 + 1, alpha)
            (x,) = _ffn_ln(x, fb_in, fb_out, l, lng, lnb, 2, alpha, False)
        return x.reshape(bsz, seq, d)

    return (trunk(x_prompt), trunk(x_sample))
```
